```python
import jax, jax.numpy as jnp
from jax import lax
import numpy as np

D_MODEL = 2048
BATCH = 4
SEQ = 4096
DEPTH = 2
DEC_BATCH = 16
DEC_SEQ = 2048
PAST_LEN = 128

HEAD_DIM = 64
D_MIX = D_MODEL
A_HEADS = 16
B_HEADS = 16
B_KV_HEADS = 4
A_WIDTH = A_HEADS * HEAD_DIM
B_WIDTH = B_HEADS * HEAD_DIM
B_KV_WIDTH = B_KV_HEADS * HEAD_DIM
IN_WIDTH = 3 * A_WIDTH + B_WIDTH + 2 * B_KV_WIDTH
IN_SPLITS = (A_WIDTH, 2 * A_WIDTH, 3 * A_WIDTH, 3 * A_WIDTH + B_WIDTH,
             3 * A_WIDTH + B_WIDTH + B_KV_WIDTH)
DILATED_PAIRS = ((128, 1), (512, 4), (2048, 16))
DILATED_BLOCK = 64
SWA_HALF = 128
SWA_BLOCK = 128
D_FF_DENSE = 5632
N_EXPERTS = 8
TOP_K = 2
D_FF_EXPERT = 7168
N_DENSE_LAYERS = (DEPTH + 1) // 2
N_MOE_LAYERS = DEPTH // 2
DEEPNORM_ALPHA = (2.0 * DEPTH) ** 0.25
DEEPNORM_BETA = (8.0 * DEPTH) ** -0.25
LN_EPS = 1e-5
NEG_INF = -1e30

kernel_name = "hymba_dilated_swa_deepnorm_adaln_encoder"


def alibi_slopes(n_heads):
    return jnp.asarray(2.0 ** (-8.0 * np.arange(1, n_heads + 1) / n_heads), jnp.float32)


def layer_norm(x, g, b):
    x32 = x.astype(jnp.float32)
    mu = x32.mean(-1, keepdims=True)
    var = jnp.square(x32 - mu).mean(-1, keepdims=True)
    return ((x32 - mu) * lax.rsqrt(var + LN_EPS) * g.astype(jnp.float32)
            + b.astype(jnp.float32)).astype(x.dtype)


def rms_norm(x, g):
    x32 = x.astype(jnp.float32)
    return x32 * lax.rsqrt(jnp.mean(jnp.square(x32), -1, keepdims=True) + LN_EPS) * g.astype(jnp.float32)


def banded_attention(q, k, v, half, blk, dist_scale, slopes, sink=None):
    N, L, H, dh = q.shape
    Hk = k.shape[2]
    G = H // Hk
    nb = -(-L // blk)
    Lp = nb * blk
    q = jnp.pad(q, ((0, 0), (0, Lp - L), (0, 0), (0, 0)))
    kp = jnp.pad(k, ((0, 0), (blk, Lp - L + blk), (0, 0), (0, 0))).reshape(N, nb + 2, blk, Hk, dh)
    vp = jnp.pad(v, ((0, 0), (blk, Lp - L + blk), (0, 0), (0, 0))).reshape(N, nb + 2, blk, Hk, dh)
    kband = jnp.concatenate([kp[:, :-2], kp[:, 1:-1], kp[:, 2:]], axis=2)
    vband = jnp.concatenate([vp[:, :-2], vp[:, 1:-1], vp[:, 2:]], axis=2)
    qb = q.reshape(N, nb, blk, Hk, G, dh)
    s = jnp.einsum('nbqkgd,nbskd->nbkgqs', qb, kband,
                   preferred_element_type=jnp.float32) * (dh ** -0.5)
    qpos = jnp.arange(Lp).reshape(nb, blk)
    kpos = (jnp.arange(nb)[:, None] - 1) * blk + jnp.arange(3 * blk)[None, :]
    dist = jnp.abs(kpos[:, None, :] - qpos[:, :, None])
    valid = (dist <= half) & (kpos[:, None, :] >= 0) & (kpos[:, None, :] < L)
    m_h = slopes.reshape(Hk, G)[None, :, :, None, None]
    s = s - m_h * (dist_scale * dist.astype(jnp.float32))[:, None, None]
    s = jnp.where(valid[:, None, None], s, NEG_INF)
    m = s.max(-1)
    if sink is not None:
        sk = sink.astype(jnp.float32).reshape(Hk, G)[None, None, :, :, None]
        m = jnp.maximum(m, sk)
    p = jnp.exp(s - m[..., None])
    l = p.sum(-1)
    if sink is not None:
        l = l + jnp.exp(sk - m)
    o = jnp.einsum('nbkgqs,nbskd->nbqkgd', p.astype(v.dtype), vband,
                   preferred_element_type=jnp.float32)
    o = o / jnp.moveaxis(l, -1, 2)[..., None]
    o = o.reshape(N, Lp, H, dh)[:, :L]
    lse = jnp.moveaxis(m + jnp.log(l), -1, 2).reshape(N, Lp, H)[:, :L]
    return o, lse


def dilated_branch(q, k, v, half_steps, r, slopes):
    S, H, dh = q.shape
    fold = lambda t: t.reshape(S // r, r, H, dh).transpose(1, 0, 2, 3)
    o, lse = banded_attention(fold(q), fold(k), fold(v), half_steps, DILATED_BLOCK, float(r), slopes)
    o = o.transpose(1, 0, 2, 3).reshape(S, H, dh)
    lse = lse.transpose(1, 0, 2).reshape(S, H)
    return o, lse


def dilated_attention(q, k, v, slopes):
    outs, lses = zip(*[dilated_branch(q, k, v, w // (2 * r), r, slopes) for (w, r) in DILATED_PAIRS])
    wts = jax.nn.softmax(jnp.stack(lses), axis=0)
    return jnp.einsum('gsh,gshd->shd', wts, jnp.stack(outs))


def mixer(h, w_in, w_out, g_out, sink, slopes):
    S = h.shape[0]
    proj = h @ w_in
    qa, ka, va, qb, kb, vb = jnp.split(proj, list(IN_SPLITS), axis=-1)
    heads = lambda t: t.reshape(S, -1, HEAD_DIM)
    oa = dilated_attention(heads(qa), heads(ka), heads(va), slopes)
    ob, _ = banded_attention(heads(qb)[None], heads(kb)[None], heads(vb)[None],
                             SWA_HALF, SWA_BLOCK, 1.0, slopes, sink)
    oa = rms_norm(oa.reshape(S, A_WIDTH), g_out[:A_WIDTH])
    ob = rms_norm(ob[0].reshape(S, B_WIDTH), g_out[A_WIDTH:])
    return jnp.concatenate([oa, ob], axis=-1).astype(h.dtype) @ w_out


def swiglu(h, w_gate, w_up, w_down):
    return (jax.nn.silu(h @ w_gate) * (h @ w_up)) @ w_down


def moe_ffn(h, w_router, w_gate, w_up, w_down):
    probs = jax.nn.softmax((h @ w_router).astype(jnp.float32), axis=-1)
    topv, topi = lax.top_k(probs, TOP_K)
    topv = topv / topv.sum(-1, keepdims=True)
    gates = jnp.sum(jax.nn.one_hot(topi, N_EXPERTS, dtype=jnp.float32) * topv[..., None], axis=1)
    y = jnp.zeros_like(h)
    for e in range(N_EXPERTS):
        y = y + gates[:, e:e + 1].astype(h.dtype) * swiglu(h, w_gate[e], w_up[e], w_down[e])
    return y


def encoder_trunk(x, c, w_ada, b_ada, w_in, w_out, g_out, attn_sink, ln1_g, ln1_b, ln2_g, ln2_b,
                  ffn_w_gate, ffn_w_up, ffn_w_down, moe_router, moe_w_gate, moe_w_up, moe_w_down):
    slopes = alibi_slopes(A_HEADS)
    for l in range(DEPTH):
        mod = jax.nn.silu(c) @ w_ada[l] + b_ada[l]

        def body(args, l=l):
            xs, ms = args
            sh1, sc1, g1, sh2, sc2, g2 = jnp.split(ms, 6)
            h = xs * (1 + sc1) + sh1
            y = mixer(h, w_in[l], w_out[l], g_out[l], attn_sink[l], slopes)
            xs = layer_norm(DEEPNORM_ALPHA * xs + g1 * y, ln1_g[l], ln1_b[l])
            h = xs * (1 + sc2) + sh2
            if l % 2 == 0:
                y = swiglu(h, ffn_w_gate[l // 2], ffn_w_up[l // 2], ffn_w_down[l // 2])
            else:
                y = moe_ffn(h, moe_router[l // 2], moe_w_gate[l // 2], moe_w_up[l // 2], moe_w_down[l // 2])
            return layer_norm(DEEPNORM_ALPHA * xs + g2 * y, ln2_g[l], ln2_b[l])

        x = lax.map(body, (x, mod))
    return x


def setup_inputs(seed: int = 0) -> dict:
    key = jax.random.key(seed)
    ks = jax.random.split(key, 24)
    nrm = lambda k, shape, scale: jax.random.normal(k, shape, jnp.float32) * scale
    col_scale = np.ones((IN_WIDTH,), np.float32)
    col_scale[2 * A_WIDTH:3 * A_WIDTH] = DEEPNORM_BETA
    col_scale[3 * A_WIDTH + B_WIDTH + B_KV_WIDTH:] = DEEPNORM_BETA
    return {
        "x_prompt": nrm(ks[0], (BATCH, SEQ, D_MODEL), 1.0),
        "x_sample": nrm(ks[1], (DEC_BATCH, DEC_SEQ, D_MODEL), 1.0),
        "c_prompt": nrm(ks[2], (BATCH, D_MODEL), 1.0),
        "c_sample": nrm(ks[3], (DEC_BATCH, D_MODEL), 1.0),
        "w_ada": nrm(ks[4], (DEPTH, D_MODEL, 6 * D_MODEL), 0.5 * D_MODEL ** -0.5),
        "b_ada": nrm(ks[5], (DEPTH, 6 * D_MODEL), 0.01),
        "w_in": nrm(ks[6], (DEPTH, D_MODEL, IN_WIDTH), D_MODEL ** -0.5) * jnp.asarray(col_scale),
        "w_out": nrm(ks[7], (DEPTH, D_MIX, D_MODEL), D_MIX ** -0.5 * DEEPNORM_BETA),
        "g_out": 1.0 + nrm(ks[8], (DEPTH, D_MIX), 0.02),
        "attn_sink": nrm(ks[9], (DEPTH, B_HEADS), 1.0),
        "ln1_g": 1.0 + nrm(ks[10], (DEPTH, D_MODEL), 0.02),
        "ln1_b": nrm(ks[11], (DEPTH, D_MODEL), 0.02),
        "ln2_g": 1.0 + nrm(ks[12], (DEPTH, D_MODEL), 0.02),
        "ln2_b": nrm(ks[13], (DEPTH, D_MODEL), 0.02),
        "ffn_w_gate": nrm(ks[14], (N_DENSE_LAYERS, D_MODEL, D_FF_DENSE), D_MODEL ** -0.5 * DEEPNORM_BETA),
        "ffn_w_up": nrm(ks[15], (N_DENSE_LAYERS, D_MODEL, D_FF_DENSE), D_MODEL ** -0.5 * DEEPNORM_BETA),
        "ffn_w_down": nrm(ks[16], (N_DENSE_LAYERS, D_FF_DENSE, D_MODEL), D_FF_DENSE ** -0.5 * DEEPNORM_BETA),
        "moe_router": nrm(ks[17], (N_MOE_LAYERS, D_MODEL, N_EXPERTS), D_MODEL ** -0.5),
        "moe_w_gate": nrm(ks[18], (N_MOE_LAYERS, N_EXPERTS, D_MODEL, D_FF_EXPERT), D_MODEL ** -0.5 * DEEPNORM_BETA),
        "moe_w_up": nrm(ks[19], (N_MOE_LAYERS, N_EXPERTS, D_MODEL, D_FF_EXPERT), D_MODEL ** -0.5 * DEEPNORM_BETA),
        "moe_w_down": nrm(ks[20], (N_MOE_LAYERS, N_EXPERTS, D_FF_EXPERT, D_MODEL), D_FF_EXPERT ** -0.5 * DEEPNORM_BETA),
    }


def reference(x_prompt, x_sample, c_prompt, c_sample, w_ada, b_ada, w_in, w_out, g_out, attn_sink,
              ln1_g, ln1_b, ln2_g, ln2_b, ffn_w_gate, ffn_w_up, ffn_w_down,
              moe_router, moe_w_gate, moe_w_up, moe_w_down):
    y_prompt = encoder_trunk(x_prompt, c_prompt, w_ada, b_ada, w_in, w_out, g_out, attn_sink,
                             ln1_g, ln1_b, ln2_g, ln2_b, ffn_w_gate, ffn_w_up, ffn_w_down,
                             moe_router, moe_w_gate, moe_w_up, moe_w_down)
    y_sample = encoder_trunk(x_sample, c_sample, w_ada, b_ada, w_in, w_out, g_out, attn_sink,
                             ln1_g, ln1_b, ln2_g, ln2_b, ffn_w_gate, ffn_w_up, ffn_w_down,
                             moe_router, moe_w_gate, moe_w_up, moe_w_down)
    return (y_prompt, y_sample)
```

```python
import functools

import jax
import jax.numpy as jnp
import numpy as np
from jax import lax
from jax.experimental import pallas as pl
from jax.experimental.pallas import tpu as pltpu

D_MODEL = 2048
DEPTH = 2
HEAD_DIM = 64
N_HEADS = 16
B_KV_HEADS = 4
A_WIDTH = N_HEADS * HEAD_DIM
B_WIDTH = N_HEADS * HEAD_DIM
B_KV_WIDTH = B_KV_HEADS * HEAD_DIM
IN_WIDTH = 3 * A_WIDTH + B_WIDTH + 2 * B_KV_WIDTH
DILATED_PAIRS = ((128, 1), (512, 4), (2048, 16))
SWA_HALF = 128
N_EXPERTS = 8
DEEPNORM_ALPHA = (2.0 * DEPTH) ** 0.25
LN_EPS = 1e-5
NEG_INF = -1e30

CHUNK = 2048
LANES = 128
HEAD_GROUP = 512
PAIRS_PER_GROUP = HEAD_GROUP // LANES
ATT_TQ = 128
ATT_QB_MAX = 512
MOE_TM = 512
VMEM_LIMIT = 56 * 1024 * 1024

_BF16 = jnp.bfloat16
_F32 = jnp.float32


def _cparams(n_axes):
    return pltpu.CompilerParams(dimension_semantics=("arbitrary",) * n_axes,
                                vmem_limit_bytes=VMEM_LIMIT)


def _layer_norm(z, g, b):
    mu = jnp.mean(z, axis=-1, keepdims=True)
    zc = z - mu
    var = jnp.mean(zc * zc, axis=-1, keepdims=True)
    return zc * lax.rsqrt(var + LN_EPS) * g + b


def _ada_kernel(c_ref, w_ref, b_ref, o_ref):
    c = c_ref[...]
    sc = (c * jax.nn.sigmoid(c)).astype(_BF16)
    w = w_ref[0].astype(_BF16)
    o_ref[0] = jnp.dot(sc, w, preferred_element_type=_F32) + b_ref[0]


def _ada(c_chunks, w_ada, b_ada):
    nch = c_chunks.shape[0]
    tn = 1024
    n_out = w_ada.shape[2]
    return pl.pallas_call(
        _ada_kernel,
        out_shape=jax.ShapeDtypeStruct((DEPTH, nch, n_out), _F32),
        grid=(DEPTH, n_out // tn),
        in_specs=[
            pl.BlockSpec((nch, D_MODEL), lambda l, j: (0, 0)),
            pl.BlockSpec((1, D_MODEL, tn), lambda l, j: (l, 0, j)),
            pl.BlockSpec((1, 1, tn), lambda l, j: (l, 0, j)),
        ],
        out_specs=pl.BlockSpec((1, nch, tn), lambda l, j: (l, 0, j)),
        compiler_params=_cparams(2),
        name="ada_mod",
    )(c_chunks, w_ada, b_ada.reshape(DEPTH, 1, n_out))


def _modulate_kernel(x_ref, sh_ref, sc_ref, o_ref):
    o_ref[...] = (x_ref[...] * (1.0 + sc_ref[0]) + sh_ref[0]).astype(o_ref.dtype)


def _mod_spec(tm, which):
    per_chunk = CHUNK // tm
    return pl.BlockSpec((1, 1, D_MODEL), lambda i, *_: (i // per_chunk, 0, which))


def _row_spec(tm, width):
    return pl.BlockSpec((tm, width), lambda i, *_: (i, 0))


def _const_spec(shape):
    return pl.BlockSpec(shape, lambda *_: (0,) * len(shape))


def _modulate(x, mod_l, tm=1024):
    t = x.shape[0]
    return pl.pallas_call(
        _modulate_kernel,
        out_shape=jax.ShapeDtypeStruct((t, D_MODEL), _BF16),
        grid=(t // tm,),
        in_specs=[_row_spec(tm, D_MODEL), _mod_spec(tm, 0), _mod_spec(tm, 1)],
        out_specs=_row_spec(tm, D_MODEL),
        compiler_params=_cparams(1),
        name="modulate",
    )(x, mod_l, mod_l)


def _matmul_kernel(x_ref, w_ref, o_ref):
    o_ref[...] = jnp.dot(x_ref[...], w_ref[...],
                         preferred_element_type=_F32).astype(o_ref.dtype)


def _matmul(x, w, tm=1024, tn=1536):
    t, k = x.shape
    n = w.shape[1]
    return pl.pallas_call(
        _matmul_kernel,
        out_shape=jax.ShapeDtypeStruct((t, n), _BF16),
        grid=(n // tn, t // tm),
        in_specs=[pl.BlockSpec((tm, k), lambda j, i: (i, 0)),
                  pl.BlockSpec((k, tn), lambda j, i: (0, j))],
        out_specs=pl.BlockSpec((tm, tn), lambda j, i: (i, j)),
        compiler_params=_cparams(2),
        name="in_proj",
    )(x, w)


def _attn_kernel(*refs, half, qb, l_fold, dist_scale, init_mode, finalize, kv_shared,
                 n_passthrough):
    it = iter(refs)
    slope_ref = next(it)
    sink_ref = next(it) if init_mode == "sink" else None
    q_ref = next(it)
    kp_ref, km_ref, kn_ref = next(it), next(it), next(it)
    vp_ref, vm_ref, vn_ref = next(it), next(it), next(it)
    if init_mode == "state":
        acc_in_ref, ml_in_ref = next(it), next(it)
    for _ in range(n_passthrough):
        next(it)
    if finalize:
        o_ref = next(it)
    else:
        acc_out_ref, ml_out_ref = next(it), next(it)
    kbuf, vbuf, bias_ref = next(it), next(it), next(it)

    tq = ATT_TQ
    win = tq + 2 * half
    n_sub = qb // tq
    hh = pl.program_id(0)
    j = pl.program_id(3)
    n_j = l_fold // qb

    @pl.when((pl.program_id(1) == 0) & (pl.program_id(2) == 0) & (j == 0))
    def _():
        row = lax.broadcasted_iota(jnp.int32, (tq, win), 0)
        col = lax.broadcasted_iota(jnp.int32, (tq, win), 1)
        dist = jnp.abs(col - half - row)
        scaled = dist.astype(_F32) * dist_scale
        band = dist <= half
        for variant in range(4):
            ok = band
            if variant & 1:
                ok = ok & (col >= half)
            if variant & 2:
                ok = ok & (col < tq + half)
            for p in range(PAIRS_PER_GROUP):
                for s in range(2):
                    slope = slope_ref[hh, 2 * p + s]
                    bias_ref[variant, p, s * tq:(s + 1) * tq, :] = jnp.where(
                        ok, -(slope * scaled), NEG_INF)

    kbuf[0:half, :] = kp_ref[...]
    kbuf[half:half + qb, :] = km_ref[...]
    kbuf[half + qb:, :] = kn_ref[...]
    vbuf[0:half, :] = vp_ref[...]
    vbuf[half:half + qb, :] = vm_ref[...]
    vbuf[half + qb:, :] = vn_ref[...]

    lane = lax.broadcasted_iota(jnp.int32, (tq, LANES), 1)
    lo = lane < HEAD_DIM
    row2 = lax.broadcasted_iota(jnp.int32, (2 * tq, 1), 0)

    def sub_block(sb, carry):
        r0 = pl.multiple_of(sb * tq, tq)
        first = (j == 0) & (sb == 0)
        last = (j == n_j - 1) & (sb == n_sub - 1)
        variant = first.astype(jnp.int32) + 2 * last.astype(jnp.int32)
        ml_tile = jnp.zeros((tq, LANES), _F32)
        if init_mode == "state":
            ml_prev = ml_in_ref[pl.ds(r0, tq), :]
        for p in range(PAIRS_PER_GROUP):
            kl = slice(0, LANES) if kv_shared else slice(p * LANES, (p + 1) * LANES)
            ql = slice(p * LANES, (p + 1) * LANES)
            q2 = q_ref[pl.ds(r0, tq), ql]
            q2 = q2 * jnp.asarray(HEAD_DIM ** -0.5, _BF16)
            zero = jnp.zeros_like(q2)
            qq = jnp.concatenate([jnp.where(lo, q2, zero), jnp.where(lo, zero, q2)], axis=0)
            kwin = kbuf[pl.ds(r0, win), kl]
            vwin = vbuf[pl.ds(r0, win), kl]
            s = lax.dot_general(qq, kwin, (((1,), (1,)), ((), ())),
                                preferred_element_type=_F32)
            s = s + bias_ref[variant, p]
            m_cur = jnp.max(s, axis=-1, keepdims=True)
            if init_mode == "none":
                m_new = m_cur
            else:
                if init_mode == "sink":
                    m_prev = jnp.where(row2 < tq, sink_ref[hh, 2 * p], sink_ref[hh, 2 * p + 1])
                    l_prev = 1.0
                else:
                    m_prev = jnp.concatenate([ml_prev[:, 2 * p:2 * p + 1],
                                              ml_prev[:, 2 * p + 1:2 * p + 2]], axis=0)
                    l_prev = jnp.concatenate([ml_prev[:, 8 + 2 * p:8 + 2 * p + 1],
                                              ml_prev[:, 8 + 2 * p + 1:8 + 2 * p + 2]], axis=0)
                m_new = jnp.maximum(m_prev, m_cur)
            pr = jnp.exp(s - m_new)
            l_new = jnp.sum(pr, axis=-1, keepdims=True)
            r = jnp.dot(pr.astype(_BF16), vwin, preferred_element_type=_F32)
            acc = jnp.where(lo, r[:tq], r[tq:])
            if init_mode != "none":
                a = jnp.exp(m_prev - m_new)
                l_new = l_prev * a + l_new
                if init_mode == "state":
                    a_lane = jnp.where(lo, a[:tq], a[tq:])
                    acc = acc_in_ref[pl.ds(r0, tq), ql] * a_lane + acc
            if finalize:
                inv = 1.0 / l_new
                o_ref[pl.ds(r0, tq), ql] = acc * jnp.where(lo, inv[:tq], inv[tq:])
            else:
                acc_out_ref[pl.ds(r0, tq), ql] = acc
                ml_tile = jnp.where(lane == 2 * p, m_new[:tq], ml_tile)
                ml_tile = jnp.where(lane == 2 * p + 1, m_new[tq:], ml_tile)
                ml_tile = jnp.where(lane == 8 + 2 * p, l_new[:tq], ml_tile)
                ml_tile = jnp.where(lane == 8 + 2 * p + 1, l_new[tq:], ml_tile)
        if not finalize:
            ml_out_ref[pl.ds(r0, tq), :] = ml_tile
        return carry

    lax.fori_loop(0, n_sub, sub_block, 0)


def _attention(proj, slopes, *, t0, n_seq, seq, r, half, q_col, k_col, v_col, kv_shared,
               sink=None, read_state, finalize, carry=None):
    t = proj.shape[0]
    l_fold = seq // r
    qb = min(l_fold, ATT_QB_MAX)
    n_j = l_fold // qb
    kw = LANES if kv_shared else HEAD_GROUP
    win = ATT_TQ + 2 * half
    base = t0 // r
    assert base % qb == 0 and l_fold % qb == 0 and qb % ATT_TQ == 0 and qb % half == 0
    proj_f = proj.reshape(t // r, r * IN_WIDTH)

    def q_map(hh, n, rho, j):
        return ((base + n * l_fold) // qb + j, (rho * IN_WIDTH + q_col) // HEAD_GROUP + hh)

    def kv_map(col, which):
        def f(hh, n, rho, j):
            if which == "main":
                blk = (base + n * l_fold) // qb + j
            elif which == "prev":
                blk = (base + n * l_fold) // half + jnp.maximum(j * (qb // half) - 1, 0)
            else:
                blk = (base + n * l_fold) // half + jnp.minimum((j + 1) * (qb // half),
                                                                l_fold // half - 1)
            return (blk, (rho * IN_WIDTH + col) // kw + hh)
        return f

    def state_map(hh, n, rho, j):
        return ((base + n * l_fold) // qb + j, rho * 2 + hh)

    smem = pl.BlockSpec(memory_space=pltpu.SMEM)
    in_specs = [smem]
    args = [slopes]
    init_mode = "none"
    if sink is not None:
        init_mode = "sink"
        in_specs.append(smem)
        args.append(sink)
    in_specs.append(pl.BlockSpec((qb, HEAD_GROUP), q_map))
    for col in (k_col, v_col):
        in_specs += [pl.BlockSpec((half, kw), kv_map(col, "prev")),
                     pl.BlockSpec((qb, kw), kv_map(col, "main")),
                     pl.BlockSpec((half, kw), kv_map(col, "next"))]
    args += [proj_f] * 7

    acc_spec = pl.BlockSpec((qb, HEAD_GROUP), state_map)
    ml_spec = pl.BlockSpec((qb, LANES), state_map)
    acc_shape = jax.ShapeDtypeStruct((t // r, r * A_WIDTH), _F32)
    ml_shape = jax.ShapeDtypeStruct((t // r, r * 2 * LANES), _F32)
    aliases = {}
    n_passthrough = 0
    if read_state:
        assert sink is None
        init_mode = "state"
        aliases[len(args)] = 0
        if not finalize:
            aliases[len(args) + 1] = 1
        in_specs += [acc_spec, ml_spec]
        args += [carry[0].reshape(acc_shape.shape), carry[1].reshape(ml_shape.shape)]
    elif carry is not None:
        n_passthrough = 1 if finalize else 2
        for k in range(n_passthrough):
            aliases[len(args)] = k
            in_specs.append(pl.BlockSpec(memory_space=pl.ANY))
            args.append(carry[k].reshape((acc_shape, ml_shape)[k].shape))
    if finalize:
        out_shape, out_specs = acc_shape, acc_spec
    else:
        out_shape, out_specs = (acc_shape, ml_shape), (acc_spec, ml_spec)

    kern = functools.partial(_attn_kernel, half=half, qb=qb, l_fold=l_fold,
                             dist_scale=float(r), init_mode=init_mode,
                             finalize=finalize, kv_shared=kv_shared,
                             n_passthrough=n_passthrough)
    out = pl.pallas_call(
        kern,
        out_shape=out_shape,
        grid=(2, n_seq, r, n_j),
        in_specs=in_specs,
        out_specs=out_specs,
        scratch_shapes=[pltpu.VMEM((qb + 2 * half, kw), _BF16),
                        pltpu.VMEM((qb + 2 * half, kw), _BF16),
                        pltpu.VMEM((4, PAIRS_PER_GROUP, 2 * ATT_TQ, win), _F32)],
        input_output_aliases=aliases,
        compiler_params=_cparams(4),
        name=f"attn_r{r}_w{half}",
    )(*args)
    if finalize:
        return out.reshape(t, A_WIDTH)
    return out[0].reshape(t, A_WIDTH), out[1].reshape(t, 2 * LANES)


def _outproj_kernel(*refs, moe):
    (oa_ref, ob_ref, ga_ref, gb_ref, w_ref, x_ref, g1_ref, sh2_ref, sc2_ref,
     lng_ref, lnb_ref) = refs[:11]
    if moe:
        wr_ref, x1_ref, h2_ref, rt_ref = refs[11:]
    else:
        x1_ref, h2_ref = refs[11:]

    def rms(o, g):
        return o * lax.rsqrt(jnp.mean(o * o, axis=-1, keepdims=True) + LN_EPS) * g

    u = jnp.concatenate([rms(oa_ref[...], ga_ref[...]), rms(ob_ref[...], gb_ref[...])],
                        axis=-1).astype(_BF16)
    y = jnp.dot(u, w_ref[...], preferred_element_type=_F32)
    x1 = _layer_norm(DEEPNORM_ALPHA * x_ref[...] + g1_ref[0] * y, lng_ref[...], lnb_ref[...])
    x1_ref[...] = x1
    h2 = x1 * (1.0 + sc2_ref[0]) + sh2_ref[0]
    h2_ref[...] = h2.astype(h2_ref.dtype)
    if moe:
        logits = jnp.dot(h2.astype(_BF16), wr_ref[...], preferred_element_type=_F32)
        lane = lax.broadcasted_iota(jnp.int32, logits.shape, 1)
        valid = lane < N_EXPERTS
        logits = jnp.where(valid, logits, NEG_INF)
        e = jnp.exp(logits - jnp.max(logits, axis=-1, keepdims=True))
        probs = jnp.where(valid, e / jnp.sum(e, axis=-1, keepdims=True), -1.0)
        p1 = jnp.max(probs, axis=-1, keepdims=True)
        i1 = jnp.min(jnp.where(probs == p1, lane, LANES), axis=-1, keepdims=True)
        rest = jnp.where(lane == i1, -1.0, probs)
        p2 = jnp.max(rest, axis=-1, keepdims=True)
        i2 = jnp.min(jnp.where(rest == p2, lane, LANES), axis=-1, keepdims=True)
        den = p1 + p2
        rt = jnp.where(lane == 0, i1.astype(_F32), 0.0)
        rt = jnp.where(lane == 1, i2.astype(_F32), rt)
        rt = jnp.where(lane == 2, p1 / den, rt)
        rt = jnp.where(lane == 3, p2 / den, rt)
        rt_ref[...] = rt


def _outproj(oa, ob, g_a, g_b, w_out, x, mod_l, ln_g, ln_b, w_router=None, tm=256):
    t = x.shape[0]
    moe = w_router is not None
    in_specs = [_row_spec(tm, A_WIDTH), _row_spec(tm, B_WIDTH),
                _const_spec((1, A_WIDTH)), _const_spec((1, B_WIDTH)),
                _const_spec((A_WIDTH + B_WIDTH, D_MODEL)), _row_spec(tm, D_MODEL),
                _mod_spec(tm, 2), _mod_spec(tm, 3), _mod_spec(tm, 4),
                _const_spec((1, D_MODEL)), _const_spec((1, D_MODEL))]
    args = [oa, ob, g_a, g_b, w_out, x, mod_l, mod_l, mod_l, ln_g, ln_b]
    out_shape = [jax.ShapeDtypeStruct((t, D_MODEL), _F32),
                 jax.ShapeDtypeStruct((t, D_MODEL), _F32 if moe else _BF16)]
    out_specs = [_row_spec(tm, D_MODEL), _row_spec(tm, D_MODEL)]
    if moe:
        in_specs.append(_const_spec((D_MODEL, LANES)))
        args.append(w_router)
        out_shape.append(jax.ShapeDtypeStruct((t, LANES), _F32))
        out_specs.append(_row_spec(tm, LANES))
    return pl.pallas_call(
        functools.partial(_outproj_kernel, moe=moe),
        out_shape=tuple(out_shape),
        grid=(t // tm,),
        in_specs=in_specs,
        out_specs=tuple(out_specs),
        compiler_params=_cparams(1),
        name="out_proj_ln",
    )(*args)


def _gateup_kernel(te_ref, tv_ref, x_ref, wg_ref, wu_ref, o_ref):
    i = pl.program_id(0)

    @pl.when(tv_ref[i] != 0)
    def _():
        x = x_ref[...]
        g = jnp.dot(x, wg_ref[0], preferred_element_type=_F32)
        u = jnp.dot(x, wu_ref[0], preferred_element_type=_F32)
        o_ref[...] = (g * jax.nn.sigmoid(g) * u).astype(o_ref.dtype)

    @pl.when(tv_ref[i] == 0)
    def _():
        o_ref[...] = jnp.zeros_like(o_ref)


def _gateup(x, w_gate, w_up, tile_expert, tile_valid, tm, tn=512):
    rows = x.shape[0]
    f = w_gate.shape[2]
    w_spec = pl.BlockSpec((1, D_MODEL, tn), lambda i, j, te, tv: (te[i], 0, j))
    return pl.pallas_call(
        _gateup_kernel,
        out_shape=jax.ShapeDtypeStruct((rows, f), _BF16),
        grid_spec=pltpu.PrefetchScalarGridSpec(
            num_scalar_prefetch=2,
            grid=(rows // tm, f // tn),
            in_specs=[pl.BlockSpec((tm, D_MODEL), lambda i, j, te, tv: (i, 0)), w_spec, w_spec],
            out_specs=pl.BlockSpec((tm, tn), lambda i, j, te, tv: (i, j)),
        ),
        compiler_params=_cparams(2),
        name="ffn_gate_up",
    )(tile_expert, tile_valid, x, w_gate, w_up)


def _down_moe_kernel(te_ref, tv_ref, a_ref, w_ref, o_ref):
    i = pl.program_id(0)
    k = pl.program_id(1)

    @pl.when(k == 0)
    def _():
        o_ref[...] = jnp.zeros_like(o_ref)

    @pl.when(tv_ref[i] != 0)
    def _():
        o_ref[...] += jnp.dot(a_ref[...], w_ref[0], preferred_element_type=_F32)


def _down_moe(act, w_down, tile_expert, tile_valid, tm, tk=1024):
    rows, f = act.shape
    return pl.pallas_call(
        _down_moe_kernel,
        out_shape=jax.ShapeDtypeStruct((rows, D_MODEL), _F32),
        grid_spec=pltpu.PrefetchScalarGridSpec(
            num_scalar_prefetch=2,
            grid=(rows // tm, f // tk),
            in_specs=[pl.BlockSpec((tm, tk), lambda i, k, te, tv: (i, k)),
                      pl.BlockSpec((1, tk, D_MODEL), lambda i, k, te, tv: (te[i], k, 0))],
            out_specs=pl.BlockSpec((tm, D_MODEL), lambda i, k, te, tv: (i, 0)),
        ),
        compiler_params=_cparams(2),
        name="moe_down",
    )(tile_expert, tile_valid, act, w_down)


def _down_ln_kernel(*refs, n_k, emit_h):
    a_ref, w_ref, x_ref, g2_ref, lng_ref, lnb_ref = refs[:6]
    if emit_h:
        sh_ref, sc_ref, x2_ref, h_ref, acc_ref = refs[6:]
    else:
        x2_ref, acc_ref = refs[6:]
    k = pl.program_id(1)

    @pl.when(k == 0)
    def _():
        acc_ref[...] = jnp.zeros_like(acc_ref)

    acc_ref[...] += jnp.dot(a_ref[...], w_ref[...], preferred_element_type=_F32)

    @pl.when(k == n_k - 1)
    def _():
        x2 = _layer_norm(DEEPNORM_ALPHA * x_ref[...] + g2_ref[0] * acc_ref[...],
                         lng_ref[...], lnb_ref[...])
        x2_ref[...] = x2
        if emit_h:
            h_ref[...] = (x2 * (1.0 + sc_ref[0]) + sh_ref[0]).astype(h_ref.dtype)


def _down_ln(act, w_down, x, mod_l, ln_g, ln_b, mod_next=None, tm=512, tk=1408):
    t, f = act.shape
    n_k = f // tk
    emit_h = mod_next is not None
    in_specs = [pl.BlockSpec((tm, tk), lambda i, k: (i, k)),
                pl.BlockSpec((tk, D_MODEL), lambda i, k: (k, 0)),
                _row_spec(tm, D_MODEL), _mod_spec(tm, 5),
                _const_spec((1, D_MODEL)), _const_spec((1, D_MODEL))]
    args = [act, w_down, x, mod_l, ln_g, ln_b]
    out_shape = [jax.ShapeDtypeStruct((t, D_MODEL), _F32)]
    out_specs = [_row_spec(tm, D_MODEL)]
    if emit_h:
        in_specs += [_mod_spec(tm, 0), _mod_spec(tm, 1)]
        args += [mod_next, mod_next]
        out_shape.append(jax.ShapeDtypeStruct((t, D_MODEL), _BF16))
        out_specs.append(_row_spec(tm, D_MODEL))
    out = pl.pallas_call(
        functools.partial(_down_ln_kernel, n_k=n_k, emit_h=emit_h),
        out_shape=tuple(out_shape),
        grid=(t // tm, n_k),
        in_specs=in_specs,
        out_specs=tuple(out_specs),
        scratch_shapes=[pltpu.VMEM((tm, D_MODEL), _F32)],
        compiler_params=_cparams(2),
        name="ffn_down_ln",
    )(*args)
    return out if emit_h else (out[0], None)


def _gather_kernel(src_ref, tv_ref, h_hbm, o_ref, buf, sem, *, tm):
    i = pl.program_id(0)

    def row_copy(k):
        return pltpu.make_async_copy(h_hbm.at[pl.ds(src_ref[i * tm + k], 1), :],
                                     buf.at[pl.ds(k, 1), :], sem)

    @pl.when(tv_ref[i] != 0)
    def _():
        def start(k, c):
            row_copy(k).start()
            return c
        lax.fori_loop(0, tm, start, 0)

        def wait(k, c):
            row_copy(k).wait()
            return c
        lax.fori_loop(0, tm, wait, 0)
        o_ref[...] = buf[...].astype(o_ref.dtype)

    @pl.when(tv_ref[i] == 0)
    def _():
        o_ref[...] = jnp.zeros_like(o_ref)


def _gather_rows(h, src_token, tile_valid, tm):
    rows = src_token.shape[0]
    return pl.pallas_call(
        functools.partial(_gather_kernel, tm=tm),
        out_shape=jax.ShapeDtypeStruct((rows, D_MODEL), _BF16),
        grid_spec=pltpu.PrefetchScalarGridSpec(
            num_scalar_prefetch=2,
            grid=(rows // tm,),
            in_specs=[pl.BlockSpec(memory_space=pl.ANY)],
            out_specs=pl.BlockSpec((tm, D_MODEL), lambda i, src, tv: (i, 0)),
            scratch_shapes=[pltpu.VMEM((tm, D_MODEL), _F32), pltpu.SemaphoreType.DMA],
        ),
        compiler_params=_cparams(1),
        name="moe_gather",
    )(src_token, tile_valid, h)


def _combine_kernel(pa_ref, pb_ref, ys_hbm, rt_ref, x_ref, g2_ref, lng_ref, lnb_ref,
                    o_ref, buf_a, buf_b, sem_a, sem_b, *, tm):
    i = pl.program_id(0)

    def copy_a(k):
        return pltpu.make_async_copy(ys_hbm.at[pl.ds(pa_ref[i * tm + k], 1), :],
                                     buf_a.at[pl.ds(k, 1), :], sem_a)

    def copy_b(k):
        return pltpu.make_async_copy(ys_hbm.at[pl.ds(pb_ref[i * tm + k], 1), :],
                                     buf_b.at[pl.ds(k, 1), :], sem_b)

    def start(k, c):
        copy_a(k).start()
        copy_b(k).start()
        return c
    lax.fori_loop(0, tm, start, 0)

    def wait(k, c):
        copy_a(k).wait()
        copy_b(k).wait()
        return c
    lax.fori_loop(0, tm, wait, 0)

    rt = rt_ref[...]
    y = rt[:, 2:3] * buf_a[...] + rt[:, 3:4] * buf_b[...]
    o_ref[...] = _layer_norm(DEEPNORM_ALPHA * x_ref[...] + g2_ref[0] * y,
                             lng_ref[...], lnb_ref[...])


def _combine_ln(ys, pos_a, pos_b, rt, x, mod_l, ln_g, ln_b, tm=256):
    t = x.shape[0]
    return pl.pallas_call(
        functools.partial(_combine_kernel, tm=tm),
        out_shape=jax.ShapeDtypeStruct((t, D_MODEL), _F32),
        grid_spec=pltpu.PrefetchScalarGridSpec(
            num_scalar_prefetch=2,
            grid=(t // tm,),
            in_specs=[pl.BlockSpec(memory_space=pl.ANY), _row_spec(tm, LANES),
                      _row_spec(tm, D_MODEL), _mod_spec(tm, 5),
                      _const_spec((1, D_MODEL)), _const_spec((1, D_MODEL))],
            out_specs=_row_spec(tm, D_MODEL),
            scratch_shapes=[pltpu.VMEM((tm, D_MODEL), _F32), pltpu.VMEM((tm, D_MODEL), _F32),
                            pltpu.SemaphoreType.DMA, pltpu.SemaphoreType.DMA],
        ),
        compiler_params=_cparams(1),
        name="moe_combine_ln",
    )(pos_a, pos_b, ys, rt, x, mod_l, ln_g, ln_b)


def _route(rt, tm):
    t = rt.shape[0]
    n_tiles = (2 * t) // tm + N_EXPERTS
    experts = rt[:, 0:2].astype(jnp.int32).reshape(-1)
    onehot = (experts[:, None] == jnp.arange(N_EXPERTS, dtype=jnp.int32)[None, :])
    csum = jnp.cumsum(onehot.astype(jnp.int32), axis=0)
    rank = jnp.take_along_axis(csum, experts[:, None], axis=1)[:, 0] - 1
    counts = csum[-1]
    tiles_per = (counts + tm - 1) // tm
    tile_end = jnp.cumsum(tiles_per)
    tile_start = tile_end - tiles_per
    pos = tile_start[experts] * tm + rank
    src_token = jnp.zeros((n_tiles * tm,), jnp.int32).at[pos].set(
        jnp.arange(2 * t, dtype=jnp.int32) // 2)
    tile_ids = jnp.arange(n_tiles, dtype=jnp.int32)
    tile_expert = jnp.minimum(
        jnp.sum((tile_ids[:, None] >= tile_end[None, :]).astype(jnp.int32), axis=1),
        N_EXPERTS - 1).astype(jnp.int32)
    tile_valid = (tile_ids < tile_end[-1]).astype(jnp.int32)
    last_expert = tile_expert[jnp.maximum(tile_end[-1] - 1, 0)]
    tile_expert = jnp.where(tile_valid != 0, tile_expert, last_expert)
    pos2 = pos.reshape(t, 2)
    return src_token, pos2[:, 0], pos2[:, 1], tile_expert, tile_valid


def _head_orders():
    perm = np.array([8 * hh + p + 4 * s for hh in range(2) for p in range(4) for s in range(2)])
    feat = (perm[:, None] * HEAD_DIM + np.arange(HEAD_DIM)[None, :]).reshape(-1)
    return perm, feat


def kernel(x_prompt, x_sample, c_prompt, c_sample, w_ada, b_ada, w_in, w_out, g_out, attn_sink,
           ln1_g, ln1_b, ln2_g, ln2_b, ffn_w_gate, ffn_w_up, ffn_w_down,
           moe_router, moe_w_gate, moe_w_up, moe_w_down):
    nb_p, s_p, _ = x_prompt.shape
    nb_s, s_s, _ = x_sample.shape
    t_p, t_s = nb_p * s_p, nb_s * s_s
    t = t_p + t_s
    groups = ((0, nb_p, s_p), (t_p, nb_s, s_s))
    for w, r in DILATED_PAIRS:
        assert w // (2 * r) == DILATED_PAIRS[0][0] // 2

    x = jnp.concatenate([x_prompt.reshape(t_p, D_MODEL), x_sample.reshape(t_s, D_MODEL)], axis=0)
    c_chunks = jnp.concatenate([jnp.repeat(c_prompt, s_p // CHUNK, axis=0),
                                jnp.repeat(c_sample, s_s // CHUNK, axis=0)], axis=0)
    nch = c_chunks.shape[0]
    mod = _ada(c_chunks, w_ada, b_ada).reshape(DEPTH, nch, 1, 6 * D_MODEL)

    perm_b, feat_b = _head_orders()
    slopes = 2.0 ** (-8.0 * np.arange(1, N_HEADS + 1) / N_HEADS)
    slopes_a = jnp.asarray(slopes.reshape(2, 8), _F32)
    slopes_b = jnp.asarray(slopes[perm_b].reshape(2, 8), _F32)
    qb0 = 3 * A_WIDTH
    in_cols = np.concatenate([np.arange(qb0), qb0 + feat_b, np.arange(qb0 + B_WIDTH, IN_WIDTH)])
    out_rows = np.concatenate([np.arange(A_WIDTH), A_WIDTH + feat_b])
    half_a = DILATED_PAIRS[0][0] // 2

    one_tile = jnp.zeros((t // 1024,), jnp.int32), jnp.ones((t // 1024,), jnp.int32)

    h = _modulate(x, mod[0])
    for l in range(DEPTH):
        mod_l = mod[l]
        w_in_l = w_in[l][:, in_cols].astype(_BF16)
        w_out_l = w_out[l][out_rows, :].astype(_BF16)
        g_l = g_out[l][out_rows]
        g_a, g_b = g_l[:A_WIDTH].reshape(1, A_WIDTH), g_l[A_WIDTH:].reshape(1, B_WIDTH)
        sink_b = attn_sink[l][perm_b].reshape(2, 8).astype(_F32)

        proj = _matmul(h, w_in_l)

        carry = None
        for bi, (_, r) in enumerate(DILATED_PAIRS):
            last = bi == len(DILATED_PAIRS) - 1
            for t0, n_seq, seq in groups:
                out = _attention(proj, slopes_a, t0=t0, n_seq=n_seq, seq=seq, r=r, half=half_a,
                                 q_col=0, k_col=A_WIDTH, v_col=2 * A_WIDTH, kv_shared=False,
                                 read_state=bi > 0, finalize=last, carry=carry)
                carry = (out, carry[1]) if last else out
        oa = carry[0]

        ob = None
        for t0, n_seq, seq in groups:
            ob = _attention(proj, slopes_b, t0=t0, n_seq=n_seq, seq=seq, r=1, half=SWA_HALF,
                            q_col=qb0, k_col=qb0 + B_WIDTH, v_col=qb0 + B_WIDTH + B_KV_WIDTH,
                            kv_shared=True, sink=sink_b, read_state=False, finalize=True,
                            carry=None if ob is None else (ob,))

        ln1 = ln1_g[l].reshape(1, D_MODEL), ln1_b[l].reshape(1, D_MODEL)
        ln2 = ln2_g[l].reshape(1, D_MODEL), ln2_b[l].reshape(1, D_MODEL)
        if l % 2 == 0:
            x, h2 = _outproj(oa, ob, g_a, g_b, w_out_l, x, mod_l, *ln1)
            e = l // 2
            act = _gateup(h2, ffn_w_gate[e:e + 1].astype(_BF16), ffn_w_up[e:e + 1].astype(_BF16),
                          *one_tile, tm=1024)
            x, h = _down_ln(act, ffn_w_down[e].astype(_BF16), x, mod_l, *ln2,
                            mod_next=mod[l + 1] if l + 1 < DEPTH else None)
        else:
            e = l // 2
            w_r = jnp.pad(moe_router[e], ((0, 0), (0, LANES - N_EXPERTS))).astype(_BF16)
            x, h2, rt = _outproj(oa, ob, g_a, g_b, w_out_l, x, mod_l, *ln1, w_router=w_r)
            src_token, pos_a, pos_b, tile_expert, tile_valid = _route(rt, MOE_TM)
            xs = _gather_rows(h2, src_token, tile_valid, MOE_TM)
            act = _gateup(xs, moe_w_gate[e].astype(_BF16), moe_w_up[e].astype(_BF16),
                          tile_expert, tile_valid, tm=MOE_TM)
            ys = _down_moe(act, moe_w_down[e].astype(_BF16), tile_expert, tile_valid, MOE_TM)
            x = _combine_ln(ys, pos_a, pos_b, rt, x, mod_l, *ln2)
            if l + 1 < DEPTH:
                h = _modulate(x, mod[l + 1])

    y_prompt = x[:t_p].reshape(nb_p, s_p, D_MODEL)
    y_sample = x[t_p:].reshape(nb_s, s_s, D_MODEL)
    return y_prompt, y_sample
```

```python
import functools

import jax
import jax.numpy as jnp
import numpy as np
from jax import lax
from jax.experimental import pallas as pl
from jax.experimental.pallas import tpu as pltpu

D_MODEL = 2048
DEPTH = 2
HEAD_DIM = 64
N_HEADS = 16
B_KV_HEADS = 4
A_WIDTH = N_HEADS * HEAD_DIM
B_WIDTH = N_HEADS * HEAD_DIM
B_KV_WIDTH = B_KV_HEADS * HEAD_DIM
IN_WIDTH = 3 * A_WIDTH + B_WIDTH + 2 * B_KV_WIDTH
DILATED_PAIRS = ((128, 1), (512, 4), (2048, 16))
DILATIONS = tuple(r for _, r in DILATED_PAIRS)
DIL_HALF = DILATED_PAIRS[0][0] // 2
DIL_HALO = DIL_HALF * max(DILATIONS)
SWA_HALF = 128
N_EXPERTS = 8
DEEPNORM_ALPHA = (2.0 * DEPTH) ** 0.25
LN_EPS = 1e-5
NEG_INF = -1e30

CHUNK = 2048
LANES = 128
HEAD_GROUP = 512
PAIRS_PER_GROUP = HEAD_GROUP // LANES
ATT_TQ = 128
DIL_UNROLL = 4
MOE_TM = 512
DMA_UNROLL = 8
VMEM_LIMIT = 56 * 1024 * 1024

_BF16 = jnp.bfloat16
_F32 = jnp.float32


def _cparams(n_axes):
    return pltpu.CompilerParams(dimension_semantics=("arbitrary",) * n_axes,
                                vmem_limit_bytes=VMEM_LIMIT)


def _layer_norm(z, g, b):
    mu = jnp.mean(z, axis=-1, keepdims=True)
    zc = z - mu
    var = jnp.mean(zc * zc, axis=-1, keepdims=True)
    return zc * lax.rsqrt(var + LN_EPS) * g + b


def _ada_kernel(c_ref, w_ref, b_ref, o_ref):
    c = c_ref[...]
    sc = (c * jax.nn.sigmoid(c)).astype(_BF16)
    w = w_ref[0].astype(_BF16)
    o_ref[0] = jnp.dot(sc, w, preferred_element_type=_F32) + b_ref[0]


def _ada(c_chunks, w_ada, b_ada):
    nch = c_chunks.shape[0]
    tn = 1024
    n_out = w_ada.shape[2]
    return pl.pallas_call(
        _ada_kernel,
        out_shape=jax.ShapeDtypeStruct((DEPTH, nch, n_out), _F32),
        grid=(DEPTH, n_out // tn),
        in_specs=[
            pl.BlockSpec((nch, D_MODEL), lambda l, j: (0, 0)),
            pl.BlockSpec((1, D_MODEL, tn), lambda l, j: (l, 0, j)),
            pl.BlockSpec((1, 1, tn), lambda l, j: (l, 0, j)),
        ],
        out_specs=pl.BlockSpec((1, nch, tn), lambda l, j: (l, 0, j)),
        compiler_params=_cparams(2),
        name="ada_mod",
    )(c_chunks, w_ada, b_ada.reshape(DEPTH, 1, n_out))


def _modulate_kernel(x_ref, sh_ref, sc_ref, o_ref):
    o_ref[...] = (x_ref[...] * (1.0 + sc_ref[0]) + sh_ref[0]).astype(o_ref.dtype)


def _mod_spec(tm, which):
    assert CHUNK % tm == 0
    per_chunk = CHUNK // tm
    return pl.BlockSpec((1, 1, D_MODEL), lambda i, *_: (i // per_chunk, 0, which))


def _row_spec(tm, width):
    return pl.BlockSpec((tm, width), lambda i, *_: (i, 0))


def _const_spec(shape):
    return pl.BlockSpec(shape, lambda *_: (0,) * len(shape))


def _modulate(x, mod_l, tm=1024):
    t = x.shape[0]
    return pl.pallas_call(
        _modulate_kernel,
        out_shape=jax.ShapeDtypeStruct((t, D_MODEL), _BF16),
        grid=(t // tm,),
        in_specs=[_row_spec(tm, D_MODEL), _mod_spec(tm, 0), _mod_spec(tm, 1)],
        out_specs=_row_spec(tm, D_MODEL),
        compiler_params=_cparams(1),
        name="modulate",
    )(x, mod_l, mod_l)


def _matmul_kernel(x_ref, w_ref, o_ref):
    o_ref[...] = jnp.dot(x_ref[...], w_ref[...],
                         preferred_element_type=_F32).astype(o_ref.dtype)


def _matmul(x, w, tm=1024, tn=1536):
    t, k = x.shape
    n = w.shape[1]
    return pl.pallas_call(
        _matmul_kernel,
        out_shape=jax.ShapeDtypeStruct((t, n), _BF16),
        grid=(n // tn, t // tm),
        in_specs=[pl.BlockSpec((tm, k), lambda j, i: (i, 0)),
                  pl.BlockSpec((k, tn), lambda j, i: (0, j))],
        out_specs=pl.BlockSpec((tm, tn), lambda j, i: (i, j)),
        compiler_params=_cparams(2),
        name="in_proj",
    )(x, w)


def _bias_tables(bias_ref, lead, slopes, half, dist_scale):
    tq = ATT_TQ
    win = tq + 2 * half
    row = lax.broadcasted_iota(jnp.int32, (tq, win), 0)
    col = lax.broadcasted_iota(jnp.int32, (tq, win), 1)
    dist = jnp.abs(col - half - row)
    scaled = dist.astype(_F32) * dist_scale
    band = dist <= half
    for variant in range(4):
        ok = band
        if variant & 1:
            ok = ok & (col >= half)
        if variant & 2:
            ok = ok & (col < tq + half)
        for s in range(2):
            bias_ref[lead + (variant, slice(s * tq, (s + 1) * tq), slice(None))] = jnp.where(
                ok, -(slopes[s] * scaled), NEG_INF)


def _pair_scores(q2, kwin, bias, lo):
    zero = jnp.zeros_like(q2)
    qq = jnp.concatenate([jnp.where(lo, q2, zero), jnp.where(lo, zero, q2)], axis=0)
    s = lax.dot_general(qq.astype(_BF16), kwin, (((1,), (1,)), ((), ())),
                        preferred_element_type=_F32)
    return s + bias


def _by_head(col, lo):
    tq = col.shape[0] // 2
    return jnp.where(lo, col[:tq], col[tq:])


def _chunk_flags(groups):
    first, last = [], []
    for _, n_seq, seq in groups:
        per = seq // CHUNK
        for _ in range(n_seq):
            first += [1] + [0] * (per - 1)
            last += [0] * (per - 1) + [1]
    return jnp.asarray(first, jnp.int32), jnp.asarray(last, jnp.int32)


def _dilated_kernel(first_ref, last_ref, slope_ref, q_ref, kp_ref, km_ref, kn_ref,
                    vp_ref, vm_ref, vn_ref, o_ref, qf, kf, vf, acc, mst, lst, bias_ref):
    tq = ATT_TQ
    half = DIL_HALF
    win = tq + 2 * half
    pair = pl.program_id(0)
    c = pl.program_id(1)

    @pl.when(c == 0)
    def _():
        slopes = (slope_ref[pair, 0], slope_ref[pair, 1])
        for b, r in enumerate(DILATIONS):
            _bias_tables(bias_ref, (b,), slopes, half, float(r))

    qf[...] = q_ref[...].astype(_F32) * (HEAD_DIM ** -0.5)
    for dst, (p_ref, m_ref, n_ref) in ((kf, (kp_ref, km_ref, kn_ref)), (vf, (vp_ref, vm_ref, vn_ref))):
        dst[0:DIL_HALO, :] = p_ref[...].astype(_F32)
        dst[DIL_HALO:DIL_HALO + CHUNK, :] = m_ref[...].astype(_F32)
        dst[DIL_HALO + CHUNK:, :] = n_ref[...].astype(_F32)

    is_first = first_ref[c] != 0
    is_last = last_ref[c] != 0
    lane = lax.broadcasted_iota(jnp.int32, (tq, LANES), 1)
    lo = lane < HEAD_DIM

    def rows(start, size, r):
        if r == 1:
            return pl.ds(pl.multiple_of(start, half), size)
        return pl.ds(start, size, stride=r)

    def local_softmax(b, r, rho, sb, n_sub):
        i0 = sb * tq
        qrows = rows(rho + r * i0, tq, r)
        krows = rows(DIL_HALO + rho + r * (i0 - half), win, r)
        variant = ((is_first & (sb == 0)).astype(jnp.int32)
                   + 2 * (is_last & (sb == n_sub - 1)).astype(jnp.int32))
        s = _pair_scores(qf[qrows, :], kf[krows, :].astype(_BF16), bias_ref[b, variant], lo)
        m_cur = jnp.max(s, axis=-1, keepdims=True)
        pr = jnp.exp(s - m_cur)
        l_cur = jnp.sum(pr, axis=-1, keepdims=True)
        pv = _by_head(jnp.dot(pr.astype(_BF16), vf[krows, :].astype(_BF16),
                              preferred_element_type=_F32), lo)
        return qrows, pv, _by_head(m_cur, lo), _by_head(l_cur, lo)

    def merge(b, qrows, pv, m_h, l_h):
        if b == 0:
            acc[qrows, :] = pv
            mst[qrows, :] = m_h
            lst[qrows, :] = l_h
            return
        m_p = mst[qrows, :]
        m_n = jnp.maximum(m_p, m_h)
        a_p = jnp.exp(m_p - m_n)
        a_c = jnp.exp(m_h - m_n)
        acc_n = acc[qrows, :] * a_p + pv * a_c
        l_n = lst[qrows, :] * a_p + l_h * a_c
        if b == len(DILATIONS) - 1:
            acc[qrows, :] = acc_n / l_n
        else:
            acc[qrows, :] = acc_n
            mst[qrows, :] = m_n
            lst[qrows, :] = l_n

    for b, r in enumerate(DILATIONS):
        n_sub = CHUNK // (r * tq)

        def body(it, carry, b=b, r=r, n_sub=n_sub):
            parts = []
            for u in range(DIL_UNROLL):
                idx = it * DIL_UNROLL + u
                parts.append(local_softmax(b, r, idx // n_sub, idx % n_sub, n_sub))
            for part in parts:
                merge(b, *part)
            return carry
        lax.fori_loop(0, r * n_sub // DIL_UNROLL, body, 0)

    o_ref[...] = acc[...].astype(o_ref.dtype)


def _dilated_attention(proj, slopes, flags):
    t = proj.shape[0]
    nch = t // CHUNK
    per = CHUNK // DIL_HALO
    n_pairs = A_WIDTH // LANES
    win = ATT_TQ + 2 * DIL_HALF

    def main(col0):
        return pl.BlockSpec((CHUNK, LANES), lambda p, c, *_: (c, col0 + p))

    def prev(col0):
        return pl.BlockSpec((DIL_HALO, LANES),
                            lambda p, c, *_: (jnp.maximum(c * per - 1, 0), col0 + p))

    def nxt(col0):
        return pl.BlockSpec((DIL_HALO, LANES),
                            lambda p, c, *_: (jnp.minimum((c + 1) * per, nch * per - 1), col0 + p))

    k0, v0 = A_WIDTH // LANES, 2 * A_WIDTH // LANES
    return pl.pallas_call(
        _dilated_kernel,
        out_shape=jax.ShapeDtypeStruct((t, A_WIDTH), _BF16),
        grid_spec=pltpu.PrefetchScalarGridSpec(
            num_scalar_prefetch=2,
            grid=(n_pairs, nch),
            in_specs=[pl.BlockSpec(memory_space=pltpu.SMEM), main(0),
                      prev(k0), main(k0), nxt(k0), prev(v0), main(v0), nxt(v0)],
            out_specs=main(0),
            scratch_shapes=[pltpu.VMEM((CHUNK, LANES), _F32),
                            pltpu.VMEM((CHUNK + 2 * DIL_HALO, LANES), _F32),
                            pltpu.VMEM((CHUNK + 2 * DIL_HALO, LANES), _F32),
                            pltpu.VMEM((CHUNK, LANES), _F32),
                            pltpu.VMEM((CHUNK, LANES), _F32),
                            pltpu.VMEM((CHUNK, LANES), _F32),
                            pltpu.VMEM((len(DILATIONS), 4, 2 * ATT_TQ, win), _F32)],
        ),
        compiler_params=_cparams(2),
        name="dilated_attn",
    )(*flags, slopes, *([proj] * 7))


def _swa_kernel(first_ref, last_ref, slope_ref, sink_ref, q_ref, kp_ref, km_ref, kn_ref,
                vp_ref, vm_ref, vn_ref, o_ref, kbuf, vbuf, bias_ref):
    tq = ATT_TQ
    half = SWA_HALF
    win = tq + 2 * half
    n_sub = CHUNK // tq
    hh = pl.program_id(0)
    c = pl.program_id(1)

    @pl.when(c == 0)
    def _():
        for p in range(PAIRS_PER_GROUP):
            _bias_tables(bias_ref, (p,), (slope_ref[hh, 2 * p], slope_ref[hh, 2 * p + 1]),
                         half, 1.0)

    for dst, (p_ref, m_ref, n_ref) in ((kbuf, (kp_ref, km_ref, kn_ref)),
                                       (vbuf, (vp_ref, vm_ref, vn_ref))):
        dst[0:half, :] = p_ref[...]
        dst[half:half + CHUNK, :] = m_ref[...]
        dst[half + CHUNK:, :] = n_ref[...]

    is_first = first_ref[c] != 0
    is_last = last_ref[c] != 0
    lane = lax.broadcasted_iota(jnp.int32, (tq, LANES), 1)
    lo = lane < HEAD_DIM
    row2 = lax.broadcasted_iota(jnp.int32, (2 * tq, 1), 0)

    def sub_block(sb, carry):
        r0 = pl.multiple_of(sb * tq, tq)
        variant = ((is_first & (sb == 0)).astype(jnp.int32)
                   + 2 * (is_last & (sb == n_sub - 1)).astype(jnp.int32))
        kwin = kbuf[pl.ds(r0, win), :]
        vwin = vbuf[pl.ds(r0, win), :]
        for p in range(PAIRS_PER_GROUP):
            ql = slice(p * LANES, (p + 1) * LANES)
            q2 = q_ref[pl.ds(r0, tq), ql] * jnp.asarray(HEAD_DIM ** -0.5, _BF16)
            s = _pair_scores(q2, kwin, bias_ref[p, variant], lo)
            sink = jnp.where(row2 < tq, sink_ref[hh, 2 * p], sink_ref[hh, 2 * p + 1])
            m_new = jnp.maximum(jnp.max(s, axis=-1, keepdims=True), sink)
            pr = jnp.exp(s - m_new)
            l_new = jnp.exp(sink - m_new) + jnp.sum(pr, axis=-1, keepdims=True)
            pv = _by_head(jnp.dot(pr.astype(_BF16), vwin, preferred_element_type=_F32), lo)
            o_ref[pl.ds(r0, tq), ql] = (pv * _by_head(1.0 / l_new, lo)).astype(o_ref.dtype)
        return carry

    lax.fori_loop(0, n_sub, sub_block, 0)


def _swa_attention(proj, slopes, sink, flags):
    t = proj.shape[0]
    nch = t // CHUNK
    per = CHUNK // SWA_HALF
    win = ATT_TQ + 2 * SWA_HALF
    q0 = 3 * A_WIDTH // HEAD_GROUP
    k0 = (3 * A_WIDTH + B_WIDTH) // LANES
    v0 = k0 + B_KV_WIDTH // LANES

    def main(col0):
        return pl.BlockSpec((CHUNK, LANES), lambda hh, c, *_: (c, col0 + hh))

    def prev(col0):
        return pl.BlockSpec((SWA_HALF, LANES),
                            lambda hh, c, *_: (jnp.maximum(c * per - 1, 0), col0 + hh))

    def nxt(col0):
        return pl.BlockSpec((SWA_HALF, LANES),
                            lambda hh, c, *_: (jnp.minimum((c + 1) * per, nch * per - 1), col0 + hh))

    smem = pl.BlockSpec(memory_space=pltpu.SMEM)
    return pl.pallas_call(
        _swa_kernel,
        out_shape=jax.ShapeDtypeStruct((t, B_WIDTH), _BF16),
        grid_spec=pltpu.PrefetchScalarGridSpec(
            num_scalar_prefetch=2,
            grid=(B_WIDTH // HEAD_GROUP, nch),
            in_specs=[smem, smem,
                      pl.BlockSpec((CHUNK, HEAD_GROUP), lambda hh, c, *_: (c, q0 + hh)),
                      prev(k0), main(k0), nxt(k0), prev(v0), main(v0), nxt(v0)],
            out_specs=pl.BlockSpec((CHUNK, HEAD_GROUP), lambda hh, c, *_: (c, hh)),
            scratch_shapes=[pltpu.VMEM((CHUNK + 2 * SWA_HALF, LANES), _BF16),
                            pltpu.VMEM((CHUNK + 2 * SWA_HALF, LANES), _BF16),
                            pltpu.VMEM((PAIRS_PER_GROUP, 4, 2 * ATT_TQ, win), _F32)],
        ),
        compiler_params=_cparams(2),
        name="swa_attn",
    )(*flags, slopes, sink, *([proj] * 7))


def _outproj_kernel(*refs, moe):
    (oa_ref, ob_ref, ga_ref, gb_ref, w_ref, x_ref, g1_ref, sh2_ref, sc2_ref,
     lng_ref, lnb_ref) = refs[:11]
    if moe:
        wr_ref, x1_ref, h2_ref, rt_ref = refs[11:]
    else:
        x1_ref, h2_ref = refs[11:]

    def rms(o_ref, g_ref):
        o = o_ref[...].astype(_F32)
        return o * lax.rsqrt(jnp.mean(o * o, axis=-1, keepdims=True) + LN_EPS) * g_ref[...]

    u = jnp.concatenate([rms(oa_ref, ga_ref), rms(ob_ref, gb_ref)], axis=-1).astype(_BF16)
    y = jnp.dot(u, w_ref[...], preferred_element_type=_F32)
    x1 = _layer_norm(DEEPNORM_ALPHA * x_ref[...] + g1_ref[0] * y, lng_ref[...], lnb_ref[...])
    x1_ref[...] = x1
    h2 = x1 * (1.0 + sc2_ref[0]) + sh2_ref[0]
    h2_ref[...] = h2.astype(h2_ref.dtype)
    if moe:
        logits = jnp.dot(h2.astype(_BF16), wr_ref[...], preferred_element_type=_F32)
        lane = lax.broadcasted_iota(jnp.int32, logits.shape, 1)
        valid = lane < N_EXPERTS
        logits = jnp.where(valid, logits, NEG_INF)
        e = jnp.exp(logits - jnp.max(logits, axis=-1, keepdims=True))
        probs = jnp.where(valid, e / jnp.sum(e, axis=-1, keepdims=True), -1.0)
        p1 = jnp.max(probs, axis=-1, keepdims=True)
        i1 = jnp.min(jnp.where(probs == p1, lane, LANES), axis=-1, keepdims=True)
        rest = jnp.where(lane == i1, -1.0, probs)
        p2 = jnp.max(rest, axis=-1, keepdims=True)
        i2 = jnp.min(jnp.where(rest == p2, lane, LANES), axis=-1, keepdims=True)
        den = p1 + p2
        rt = jnp.where(lane == 0, i1.astype(_F32), 0.0)
        rt = jnp.where(lane == 1, i2.astype(_F32), rt)
        rt = jnp.where(lane == 2, p1 / den, rt)
        rt = jnp.where(lane == 3, p2 / den, rt)
        rt_ref[...] = rt


def _outproj(oa, ob, g_a, g_b, w_out, x, mod_l, ln_g, ln_b, w_router=None, tm=256):
    t = x.shape[0]
    moe = w_router is not None
    in_specs = [_row_spec(tm, A_WIDTH), _row_spec(tm, B_WIDTH),
                _const_spec((1, A_WIDTH)), _const_spec((1, B_WIDTH)),
                _const_spec((A_WIDTH + B_WIDTH, D_MODEL)), _row_spec(tm, D_MODEL),
                _mod_spec(tm, 2), _mod_spec(tm, 3), _mod_spec(tm, 4),
                _const_spec((1, D_MODEL)), _const_spec((1, D_MODEL))]
    args = [oa, ob, g_a, g_b, w_out, x, mod_l, mod_l, mod_l, ln_g, ln_b]
    out_shape = [jax.ShapeDtypeStruct((t, D_MODEL), _F32),
                 jax.ShapeDtypeStruct((t, D_MODEL), _F32 if moe else _BF16)]
    out_specs = [_row_spec(tm, D_MODEL), _row_spec(tm, D_MODEL)]
    if moe:
        in_specs.append(_const_spec((D_MODEL, LANES)))
        args.append(w_router)
        out_shape.append(jax.ShapeDtypeStruct((t, LANES), _F32))
        out_specs.append(_row_spec(tm, LANES))
    return pl.pallas_call(
        functools.partial(_outproj_kernel, moe=moe),
        out_shape=tuple(out_shape),
        grid=(t // tm,),
        in_specs=in_specs,
        out_specs=tuple(out_specs),
        compiler_params=_cparams(1),
        name="out_proj_ln",
    )(*args)


def _gateup_kernel(te_ref, tv_ref, x_ref, wg_ref, wu_ref, o_ref):
    i = pl.program_id(0)

    @pl.when(tv_ref[i] != 0)
    def _():
        x = x_ref[...]
        g = jnp.dot(x, wg_ref[0], preferred_element_type=_F32)
        u = jnp.dot(x, wu_ref[0], preferred_element_type=_F32)
        o_ref[...] = (g * jax.nn.sigmoid(g) * u).astype(o_ref.dtype)

    @pl.when(tv_ref[i] == 0)
    def _():
        o_ref[...] = jnp.zeros_like(o_ref)


def _gateup(x, w_gate, w_up, tile_expert, tile_valid, tm, tn=512):
    rows = x.shape[0]
    f = w_gate.shape[2]
    w_spec = pl.BlockSpec((1, D_MODEL, tn), lambda i, j, te, tv: (te[i], 0, j))
    return pl.pallas_call(
        _gateup_kernel,
        out_shape=jax.ShapeDtypeStruct((rows, f), _BF16),
        grid_spec=pltpu.PrefetchScalarGridSpec(
            num_scalar_prefetch=2,
            grid=(rows // tm, f // tn),
            in_specs=[pl.BlockSpec((tm, D_MODEL), lambda i, j, te, tv: (i, 0)), w_spec, w_spec],
            out_specs=pl.BlockSpec((tm, tn), lambda i, j, te, tv: (i, j)),
        ),
        compiler_params=_cparams(2),
        name="ffn_gate_up",
    )(tile_expert, tile_valid, x, w_gate, w_up)


def _down_moe_kernel(te_ref, tv_ref, a_ref, w_ref, o_ref):
    i = pl.program_id(0)
    k = pl.program_id(1)

    @pl.when(k == 0)
    def _():
        o_ref[...] = jnp.zeros_like(o_ref)

    @pl.when(tv_ref[i] != 0)
    def _():
        o_ref[...] += jnp.dot(a_ref[...], w_ref[0], preferred_element_type=_F32)


def _down_moe(act, w_down, tile_expert, tile_valid, tm, tk=1024):
    rows, f = act.shape
    return pl.pallas_call(
        _down_moe_kernel,
        out_shape=jax.ShapeDtypeStruct((rows, D_MODEL), _F32),
        grid_spec=pltpu.PrefetchScalarGridSpec(
            num_scalar_prefetch=2,
            grid=(rows // tm, f // tk),
            in_specs=[pl.BlockSpec((tm, tk), lambda i, k, te, tv: (i, k)),
                      pl.BlockSpec((1, tk, D_MODEL), lambda i, k, te, tv: (te[i], k, 0))],
            out_specs=pl.BlockSpec((tm, D_MODEL), lambda i, k, te, tv: (i, 0)),
        ),
        compiler_params=_cparams(2),
        name="moe_down",
    )(tile_expert, tile_valid, act, w_down)


def _down_ln_kernel(*refs, n_k, emit_h):
    a_ref, w_ref, x_ref, g2_ref, lng_ref, lnb_ref = refs[:6]
    if emit_h:
        sh_ref, sc_ref, x2_ref, h_ref = refs[6:]
    else:
        (x2_ref,) = refs[6:]
    k = pl.program_id(1)
    part = jnp.dot(a_ref[...], w_ref[...], preferred_element_type=_F32)

    @pl.when(k == 0)
    def _():
        x2_ref[...] = part

    @pl.when((k > 0) & (k < n_k - 1))
    def _():
        x2_ref[...] += part

    @pl.when(k == n_k - 1)
    def _():
        y = x2_ref[...] + part
        x2 = _layer_norm(DEEPNORM_ALPHA * x_ref[...] + g2_ref[0] * y, lng_ref[...], lnb_ref[...])
        x2_ref[...] = x2
        if emit_h:
            h_ref[...] = (x2 * (1.0 + sc_ref[0]) + sh_ref[0]).astype(h_ref.dtype)


def _down_ln(act, w_down, x, mod_l, ln_g, ln_b, mod_next=None, tm=512, tk=1408):
    t, f = act.shape
    n_k = f // tk
    assert n_k >= 2
    emit_h = mod_next is not None
    in_specs = [pl.BlockSpec((tm, tk), lambda i, k: (i, k)),
                pl.BlockSpec((tk, D_MODEL), lambda i, k: (k, 0)),
                _row_spec(tm, D_MODEL), _mod_spec(tm, 5),
                _const_spec((1, D_MODEL)), _const_spec((1, D_MODEL))]
    args = [act, w_down, x, mod_l, ln_g, ln_b]
    out_shape = [jax.ShapeDtypeStruct((t, D_MODEL), _F32)]
    out_specs = [_row_spec(tm, D_MODEL)]
    if emit_h:
        in_specs += [_mod_spec(tm, 0), _mod_spec(tm, 1)]
        args += [mod_next, mod_next]
        out_shape.append(jax.ShapeDtypeStruct((t, D_MODEL), _BF16))
        out_specs.append(_row_spec(tm, D_MODEL))
    out = pl.pallas_call(
        functools.partial(_down_ln_kernel, n_k=n_k, emit_h=emit_h),
        out_shape=tuple(out_shape),
        grid=(t // tm, n_k),
        in_specs=in_specs,
        out_specs=tuple(out_specs),
        compiler_params=_cparams(2),
        name="ffn_down_ln",
    )(*args)
    return out if emit_h else (out[0], None)


def _start_row_copies(idx_ref, base, src_hbm, buf, sem, tm):
    def body(kk, c):
        for u in range(DMA_UNROLL):
            k = kk * DMA_UNROLL + u
            pltpu.make_async_copy(src_hbm.at[pl.ds(idx_ref[base + k], 1), :],
                                  buf.at[pl.ds(k, 1), :], sem).start()
        return c
    lax.fori_loop(0, tm // DMA_UNROLL, body, 0)


def _wait_row_copies(src_hbm, buf, sem, tm):
    pltpu.make_async_copy(src_hbm.at[pl.ds(0, tm), :], buf, sem).wait()


def _gather_kernel(src_ref, tv_ref, h_hbm, o_ref, buf, sem, *, tm):
    i = pl.program_id(0)

    @pl.when(tv_ref[i] != 0)
    def _():
        _start_row_copies(src_ref, i * tm, h_hbm, buf, sem, tm)
        _wait_row_copies(h_hbm, buf, sem, tm)
        o_ref[...] = buf[...].astype(o_ref.dtype)

    @pl.when(tv_ref[i] == 0)
    def _():
        o_ref[...] = jnp.zeros_like(o_ref)


def _gather_rows(h, src_token, tile_valid, tm):
    rows = src_token.shape[0]
    return pl.pallas_call(
        functools.partial(_gather_kernel, tm=tm),
        out_shape=jax.ShapeDtypeStruct((rows, D_MODEL), _BF16),
        grid_spec=pltpu.PrefetchScalarGridSpec(
            num_scalar_prefetch=2,
            grid=(rows // tm,),
            in_specs=[pl.BlockSpec(memory_space=pl.ANY)],
            out_specs=pl.BlockSpec((tm, D_MODEL), lambda i, src, tv: (i, 0)),
            scratch_shapes=[pltpu.VMEM((tm, D_MODEL), _F32), pltpu.SemaphoreType.DMA],
        ),
        compiler_params=_cparams(1),
        name="moe_gather",
    )(src_token, tile_valid, h)


def _combine_kernel(pa_ref, pb_ref, ys_hbm, rt_ref, x_ref, g2_ref, lng_ref, lnb_ref,
                    o_ref, buf_a, buf_b, sem_a, sem_b, *, tm):
    i = pl.program_id(0)
    _start_row_copies(pa_ref, i * tm, ys_hbm, buf_a, sem_a, tm)
    _start_row_copies(pb_ref, i * tm, ys_hbm, buf_b, sem_b, tm)
    _wait_row_copies(ys_hbm, buf_a, sem_a, tm)
    _wait_row_copies(ys_hbm, buf_b, sem_b, tm)
    rt = rt_ref[...]
    y = rt[:, 2:3] * buf_a[...] + rt[:, 3:4] * buf_b[...]
    o_ref[...] = _layer_norm(DEEPNORM_ALPHA * x_ref[...] + g2_ref[0] * y,
                             lng_ref[...], lnb_ref[...])


def _combine_ln(ys, pos_a, pos_b, rt, x, mod_l, ln_g, ln_b, tm=256):
    t = x.shape[0]
    return pl.pallas_call(
        functools.partial(_combine_kernel, tm=tm),
        out_shape=jax.ShapeDtypeStruct((t, D_MODEL), _F32),
        grid_spec=pltpu.PrefetchScalarGridSpec(
            num_scalar_prefetch=2,
            grid=(t // tm,),
            in_specs=[pl.BlockSpec(memory_space=pl.ANY), _row_spec(tm, LANES),
                      _row_spec(tm, D_MODEL), _mod_spec(tm, 5),
                      _const_spec((1, D_MODEL)), _const_spec((1, D_MODEL))],
            out_specs=_row_spec(tm, D_MODEL),
            scratch_shapes=[pltpu.VMEM((tm, D_MODEL), _F32), pltpu.VMEM((tm, D_MODEL), _F32),
                            pltpu.SemaphoreType.DMA, pltpu.SemaphoreType.DMA],
        ),
        compiler_params=_cparams(1),
        name="moe_combine_ln",
    )(pos_a, pos_b, ys, rt, x, mod_l, ln_g, ln_b)


def _route(rt, tm):
    t = rt.shape[0]
    n_tiles = (2 * t) // tm + N_EXPERTS
    experts = rt[:, 0:2].astype(jnp.int32).reshape(-1)
    onehot = (experts[:, None] == jnp.arange(N_EXPERTS, dtype=jnp.int32)[None, :])
    csum = jnp.cumsum(onehot.astype(jnp.int32), axis=0)
    rank = jnp.take_along_axis(csum, experts[:, None], axis=1)[:, 0] - 1
    counts = csum[-1]
    tiles_per = (counts + tm - 1) // tm
    tile_end = jnp.cumsum(tiles_per)
    tile_start = tile_end - tiles_per
    pos = tile_start[experts] * tm + rank
    src_token = jnp.zeros((n_tiles * tm,), jnp.int32).at[pos].set(
        jnp.arange(2 * t, dtype=jnp.int32) // 2)
    tile_ids = jnp.arange(n_tiles, dtype=jnp.int32)
    tile_expert = jnp.minimum(
        jnp.sum((tile_ids[:, None] >= tile_end[None, :]).astype(jnp.int32), axis=1),
        N_EXPERTS - 1).astype(jnp.int32)
    tile_valid = (tile_ids < tile_end[-1]).astype(jnp.int32)
    last_expert = tile_expert[jnp.maximum(tile_end[-1] - 1, 0)]
    tile_expert = jnp.where(tile_valid != 0, tile_expert, last_expert)
    pos2 = pos.reshape(t, 2)
    return src_token, pos2[:, 0], pos2[:, 1], tile_expert, tile_valid


def _head_orders():
    perm = np.array([8 * hh + p + 4 * s for hh in range(2) for p in range(4) for s in range(2)])
    feat = (perm[:, None] * HEAD_DIM + np.arange(HEAD_DIM)[None, :]).reshape(-1)
    return perm, feat


def kernel(x_prompt, x_sample, c_prompt, c_sample, w_ada, b_ada, w_in, w_out, g_out, attn_sink,
           ln1_g, ln1_b, ln2_g, ln2_b, ffn_w_gate, ffn_w_up, ffn_w_down,
           moe_router, moe_w_gate, moe_w_up, moe_w_down):
    nb_p, s_p, _ = x_prompt.shape
    nb_s, s_s, _ = x_sample.shape
    t_p, t_s = nb_p * s_p, nb_s * s_s
    t = t_p + t_s
    groups = ((0, nb_p, s_p), (t_p, nb_s, s_s))
    for w, r in DILATED_PAIRS:
        assert w // (2 * r) == DIL_HALF and CHUNK % (r * ATT_TQ) == 0
    assert s_p % CHUNK == 0 and s_s % CHUNK == 0 and CHUNK % DIL_HALO == 0

    x = jnp.concatenate([x_prompt.reshape(t_p, D_MODEL), x_sample.reshape(t_s, D_MODEL)], axis=0)
    c_chunks = jnp.concatenate([jnp.repeat(c_prompt, s_p // CHUNK, axis=0),
                                jnp.repeat(c_sample, s_s // CHUNK, axis=0)], axis=0)
    nch = c_chunks.shape[0]
    mod = _ada(c_chunks, w_ada, b_ada).reshape(DEPTH, nch, 1, 6 * D_MODEL)
    flags = _chunk_flags(groups)

    perm_b, feat_b = _head_orders()
    slopes = 2.0 ** (-8.0 * np.arange(1, N_HEADS + 1) / N_HEADS)
    slopes_a = jnp.asarray(slopes.reshape(N_HEADS // 2, 2), _F32)
    slopes_b = jnp.asarray(slopes[perm_b].reshape(2, 8), _F32)
    qb0 = 3 * A_WIDTH
    in_cols = np.concatenate([np.arange(qb0), qb0 + feat_b, np.arange(qb0 + B_WIDTH, IN_WIDTH)])
    out_rows = np.concatenate([np.arange(A_WIDTH), A_WIDTH + feat_b])

    one_tile = jnp.zeros((t // 1024,), jnp.int32), jnp.ones((t // 1024,), jnp.int32)

    h = _modulate(x, mod[0])
    for l in range(DEPTH):
        mod_l = mod[l]
        w_in_l = w_in[l][:, in_cols].astype(_BF16)
        w_out_l = w_out[l][out_rows, :].astype(_BF16)
        g_l = g_out[l][out_rows]
        g_a, g_b = g_l[:A_WIDTH].reshape(1, A_WIDTH), g_l[A_WIDTH:].reshape(1, B_WIDTH)
        sink_b = attn_sink[l][perm_b].reshape(2, 8).astype(_F32)

        proj = _matmul(h, w_in_l)
        oa = _dilated_attention(proj, slopes_a, flags)
        ob = _swa_attention(proj, slopes_b, sink_b, flags)

        ln1 = ln1_g[l].reshape(1, D_MODEL), ln1_b[l].reshape(1, D_MODEL)
        ln2 = ln2_g[l].reshape(1, D_MODEL), ln2_b[l].reshape(1, D_MODEL)
        e = l // 2
        if l % 2 == 0:
            x, h2 = _outproj(oa, ob, g_a, g_b, w_out_l, x, mod_l, *ln1)
            act = _gateup(h2, ffn_w_gate[e:e + 1].astype(_BF16), ffn_w_up[e:e + 1].astype(_BF16),
                          *one_tile, tm=1024)
            x, h = _down_ln(act, ffn_w_down[e].astype(_BF16), x, mod_l, *ln2,
                            mod_next=mod[l + 1] if l + 1 < DEPTH else None)
        else:
            w_r = jnp.pad(moe_router[e], ((0, 0), (0, LANES - N_EXPERTS))).astype(_BF16)
            x, h2, rt = _outproj(oa, ob, g_a, g_b, w_out_l, x, mod_l, *ln1, w_router=w_r)
            src_token, pos_a, pos_b, tile_expert, tile_valid = _route(rt, MOE_TM)
            xs = _gather_rows(h2, src_token, tile_valid, MOE_TM)
            act = _gateup(xs, moe_w_gate[e].astype(_BF16), moe_w_up[e].astype(_BF16),
                          tile_expert, tile_valid, tm=MOE_TM)
            ys = _down_moe(act, moe_w_down[e].astype(_BF16), tile_expert, tile_valid, MOE_TM)
            x = _combine_ln(ys, pos_a, pos_b, rt, x, mod_l, *ln2)
            if l + 1 < DEPTH:
                h = _modulate(x, mod[l + 1])

    y_prompt = x[:t_p].reshape(nb_p, s_p, D_MODEL)
    y_sample = x[t_p:].reshape(nb_s, s_s, D_MODEL)
    return y_prompt, y_sample
```

```python
import functools

import jax
import jax.numpy as jnp
import numpy as np
from jax import lax
from jax.experimental import pallas as pl
from jax.experimental.pallas import tpu as pltpu

D_MODEL = 2048
DEPTH = 2
HEAD_DIM = 64
N_HEADS = 16
B_KV_HEADS = 4
A_WIDTH = N_HEADS * HEAD_DIM
B_WIDTH = N_HEADS * HEAD_DIM
B_KV_WIDTH = B_KV_HEADS * HEAD_DIM
IN_WIDTH = 3 * A_WIDTH + B_WIDTH + 2 * B_KV_WIDTH
DILATED_PAIRS = ((128, 1), (512, 4), (2048, 16))
DILATIONS = tuple(sorted((r for _, r in DILATED_PAIRS), reverse=True))
assert DILATIONS[-1] == 1
DIL_HALF = DILATED_PAIRS[0][0] // 2
DIL_HALO = DIL_HALF * max(DILATIONS)
SWA_HALF = 128
N_EXPERTS = 8
DEEPNORM_ALPHA = (2.0 * DEPTH) ** 0.25
LN_EPS = 1e-5
NEG_INF = -1e30

CHUNK = 2048
LANES = 128
HEAD_GROUP = 512
PAIRS_PER_GROUP = HEAD_GROUP // LANES
ATT_TQ = 128
DIL_UNROLL = 8
SWA_UNROLL = 2
MOE_TM = 512
FFN_TM = 512
DMA_UNROLL = 8
VMEM_LIMIT = 56 * 1024 * 1024

_BF16 = jnp.bfloat16
_F32 = jnp.float32


def _cparams(n_axes):
    return pltpu.CompilerParams(dimension_semantics=("arbitrary",) * n_axes,
                                vmem_limit_bytes=VMEM_LIMIT)


def _layer_norm(z, g, b):
    mu = jnp.mean(z, axis=-1, keepdims=True)
    zc = z - mu
    var = jnp.mean(zc * zc, axis=-1, keepdims=True)
    return zc * lax.rsqrt(var + LN_EPS) * g + b


def _ada_kernel(c_ref, w_ref, b_ref, o_ref):
    c = c_ref[...]
    sc = (c * jax.nn.sigmoid(c)).astype(_BF16)
    w = w_ref[0].astype(_BF16)
    o_ref[0] = jnp.dot(sc, w, preferred_element_type=_F32) + b_ref[0]


def _ada(c_chunks, w_ada, b_ada):
    nch = c_chunks.shape[0]
    tn = 1024
    n_out = w_ada.shape[2]
    return pl.pallas_call(
        _ada_kernel,
        out_shape=jax.ShapeDtypeStruct((DEPTH, nch, n_out), _F32),
        grid=(DEPTH, n_out // tn),
        in_specs=[
            pl.BlockSpec((nch, D_MODEL), lambda l, j: (0, 0)),
            pl.BlockSpec((1, D_MODEL, tn), lambda l, j: (l, 0, j)),
            pl.BlockSpec((1, 1, tn), lambda l, j: (l, 0, j)),
        ],
        out_specs=pl.BlockSpec((1, nch, tn), lambda l, j: (l, 0, j)),
        compiler_params=_cparams(2),
        name="ada_mod",
    )(c_chunks, w_ada, b_ada.reshape(DEPTH, 1, n_out))


def _modulate_kernel(x_ref, sh_ref, sc_ref, o_ref):
    o_ref[...] = (x_ref[...] * (1.0 + sc_ref[0]) + sh_ref[0]).astype(o_ref.dtype)


def _mod_spec(tm, which):
    assert CHUNK % tm == 0
    per_chunk = CHUNK // tm
    return pl.BlockSpec((1, 1, D_MODEL), lambda i, *_: (i // per_chunk, 0, which))


def _row_spec(tm, width):
    return pl.BlockSpec((tm, width), lambda i, *_: (i, 0))


def _const_spec(shape):
    return pl.BlockSpec(shape, lambda *_: (0,) * len(shape))


def _modulate(x, mod_l, tm=1024):
    t = x.shape[0]
    return pl.pallas_call(
        _modulate_kernel,
        out_shape=jax.ShapeDtypeStruct((t, D_MODEL), _BF16),
        grid=(t // tm,),
        in_specs=[_row_spec(tm, D_MODEL), _mod_spec(tm, 0), _mod_spec(tm, 1)],
        out_specs=_row_spec(tm, D_MODEL),
        compiler_params=_cparams(1),
        name="modulate",
    )(x, mod_l, mod_l)


def _matmul_kernel(x_ref, w_ref, o_ref):
    o_ref[...] = jnp.dot(x_ref[...], w_ref[...],
                         preferred_element_type=_F32).astype(o_ref.dtype)


def _matmul(x, w, tm=1024, tn=1536):
    t, k = x.shape
    n = w.shape[1]
    return pl.pallas_call(
        _matmul_kernel,
        out_shape=jax.ShapeDtypeStruct((t, n), _BF16),
        grid=(n // tn, t // tm),
        in_specs=[pl.BlockSpec((tm, k), lambda j, i: (i, 0)),
                  pl.BlockSpec((k, tn), lambda j, i: (0, j))],
        out_specs=pl.BlockSpec((tm, tn), lambda j, i: (i, j)),
        compiler_params=_cparams(2),
        name="in_proj",
    )(x, w)


def _bias_tables(bias_ref, lead, slopes, half, dist_scale):
    tq = ATT_TQ
    win = tq + 2 * half
    row = lax.broadcasted_iota(jnp.int32, (tq, win), 0)
    col = lax.broadcasted_iota(jnp.int32, (tq, win), 1)
    dist = jnp.abs(col - half - row)
    scaled = dist.astype(_F32) * dist_scale
    band = dist <= half
    for variant in range(4):
        ok = band
        if variant & 1:
            ok = ok & (col >= half)
        if variant & 2:
            ok = ok & (col < tq + half)
        for s in range(2):
            bias_ref[lead + (variant, slice(s * tq, (s + 1) * tq), slice(None))] = jnp.where(
                ok, -(slopes[s] * scaled), NEG_INF)


def _pair_scores(q2, kwin, bias, lo):
    zero = jnp.zeros_like(q2)
    qq = jnp.concatenate([jnp.where(lo, q2, zero), jnp.where(lo, zero, q2)], axis=0)
    s = lax.dot_general(qq.astype(_BF16), kwin, (((1,), (1,)), ((), ())),
                        preferred_element_type=_F32)
    return s + bias


def _by_head(col, lo):
    tq = col.shape[0] // 2
    return jnp.where(lo, col[:tq], col[tq:])


def _chunk_flags(groups):
    first, last = [], []
    for _, n_seq, seq in groups:
        per = seq // CHUNK
        for _ in range(n_seq):
            first += [1] + [0] * (per - 1)
            last += [0] * (per - 1) + [1]
    return jnp.asarray(first, jnp.int32), jnp.asarray(last, jnp.int32)


def _dilated_kernel(first_ref, last_ref, slope_ref, q_ref, kp_ref, km_ref, kn_ref,
                    vp_ref, vm_ref, vn_ref, o_ref, qf, kf, vf, acc, mst, lst, bias_ref):
    tq = ATT_TQ
    half = DIL_HALF
    win = tq + 2 * half
    pair = pl.program_id(0)
    c = pl.program_id(1)

    @pl.when(c == 0)
    def _():
        slopes = (slope_ref[pair, 0], slope_ref[pair, 1])
        for b, r in enumerate(DILATIONS):
            _bias_tables(bias_ref, (b,), slopes, half, float(r))

    qf[...] = q_ref[...].astype(_F32) * (HEAD_DIM ** -0.5)
    for dst, (p_ref, m_ref, n_ref) in ((kf, (kp_ref, km_ref, kn_ref)), (vf, (vp_ref, vm_ref, vn_ref))):
        dst[0:DIL_HALO, :] = p_ref[...].astype(_F32)
        dst[DIL_HALO:DIL_HALO + CHUNK, :] = m_ref[...].astype(_F32)
        dst[DIL_HALO + CHUNK:, :] = n_ref[...].astype(_F32)

    is_first = first_ref[c] != 0
    is_last = last_ref[c] != 0
    lane = lax.broadcasted_iota(jnp.int32, (tq, LANES), 1)
    lo = lane < HEAD_DIM

    def rows(start, size, r):
        if r == 1:
            return pl.ds(pl.multiple_of(start, half), size)
        return pl.ds(start, size, stride=r)

    def local_softmax(b, r, rho, sb, n_sub):
        i0 = sb * tq
        qrows = rows(rho + r * i0, tq, r)
        krows = rows(DIL_HALO + rho + r * (i0 - half), win, r)
        variant = ((is_first & (sb == 0)).astype(jnp.int32)
                   + 2 * (is_last & (sb == n_sub - 1)).astype(jnp.int32))
        s = _pair_scores(qf[qrows, :], kf[krows, :].astype(_BF16), bias_ref[b, variant], lo)
        m_cur = jnp.max(s, axis=-1, keepdims=True)
        pr = jnp.exp(s - m_cur)
        l_cur = jnp.sum(pr, axis=-1, keepdims=True)
        pv = _by_head(jnp.dot(pr.astype(_BF16), vf[krows, :].astype(_BF16),
                              preferred_element_type=_F32), lo)
        return qrows, pv, _by_head(m_cur, lo), _by_head(l_cur, lo)

    def merge(b, qrows, pv, m_h, l_h):
        if b == 0:
            acc[qrows, :] = pv
            mst[qrows, :] = m_h
            lst[qrows, :] = l_h
            return
        m_p = mst[qrows, :]
        m_n = jnp.maximum(m_p, m_h)
        a_p = jnp.exp(m_p - m_n)
        a_c = jnp.exp(m_h - m_n)
        acc_n = acc[qrows, :] * a_p + pv * a_c
        l_n = lst[qrows, :] * a_p + l_h * a_c
        if b == len(DILATIONS) - 1:
            o_ref[qrows, :] = (acc_n / l_n).astype(o_ref.dtype)
        else:
            acc[qrows, :] = acc_n
            mst[qrows, :] = m_n
            lst[qrows, :] = l_n

    for b, r in enumerate(DILATIONS):
        n_sub = CHUNK // (r * tq)

        def body(it, carry, b=b, r=r, n_sub=n_sub):
            parts = []
            for u in range(DIL_UNROLL):
                idx = it * DIL_UNROLL + u
                parts.append(local_softmax(b, r, idx // n_sub, idx % n_sub, n_sub))
            for part in parts:
                merge(b, *part)
            return carry
        lax.fori_loop(0, r * n_sub // DIL_UNROLL, body, 0)


def _dilated_attention(proj, slopes, flags):
    t = proj.shape[0]
    nch = t // CHUNK
    per = CHUNK // DIL_HALO
    n_pairs = A_WIDTH // LANES
    win = ATT_TQ + 2 * DIL_HALF

    def main(col0):
        return pl.BlockSpec((CHUNK, LANES), lambda p, c, *_: (c, col0 + p))

    def prev(col0):
        return pl.BlockSpec((DIL_HALO, LANES),
                            lambda p, c, *_: (jnp.maximum(c * per - 1, 0), col0 + p))

    def nxt(col0):
        return pl.BlockSpec((DIL_HALO, LANES),
                            lambda p, c, *_: (jnp.minimum((c + 1) * per, nch * per - 1), col0 + p))

    k0, v0 = A_WIDTH // LANES, 2 * A_WIDTH // LANES
    return pl.pallas_call(
        _dilated_kernel,
        out_shape=jax.ShapeDtypeStruct((t, A_WIDTH), _BF16),
        grid_spec=pltpu.PrefetchScalarGridSpec(
            num_scalar_prefetch=2,
            grid=(n_pairs, nch),
            in_specs=[pl.BlockSpec(memory_space=pltpu.SMEM), main(0),
                      prev(k0), main(k0), nxt(k0), prev(v0), main(v0), nxt(v0)],
            out_specs=main(0),
            scratch_shapes=[pltpu.VMEM((CHUNK, LANES), _F32),
                            pltpu.VMEM((CHUNK + 2 * DIL_HALO, LANES), _F32),
                            pltpu.VMEM((CHUNK + 2 * DIL_HALO, LANES), _F32),
                            pltpu.VMEM((CHUNK, LANES), _F32),
                            pltpu.VMEM((CHUNK, LANES), _F32),
                            pltpu.VMEM((CHUNK, LANES), _F32),
                            pltpu.VMEM((len(DILATIONS), 4, 2 * ATT_TQ, win), _F32)],
        ),
        compiler_params=_cparams(2),
        name="dilated_attn",
    )(*flags, slopes, *([proj] * 7))


def _swa_kernel(first_ref, last_ref, slope_ref, sink_ref, q_ref, kp_ref, km_ref, kn_ref,
                vp_ref, vm_ref, vn_ref, o_ref, kbuf, vbuf, bias_ref):
    tq = ATT_TQ
    half = SWA_HALF
    win = tq + 2 * half
    n_sub = CHUNK // tq
    hh = pl.program_id(0)
    c = pl.program_id(1)

    @pl.when(c == 0)
    def _():
        for p in range(PAIRS_PER_GROUP):
            _bias_tables(bias_ref, (p,), (slope_ref[hh, 2 * p], slope_ref[hh, 2 * p + 1]),
                         half, 1.0)

    for dst, (p_ref, m_ref, n_ref) in ((kbuf, (kp_ref, km_ref, kn_ref)),
                                       (vbuf, (vp_ref, vm_ref, vn_ref))):
        dst[0:half, :] = p_ref[...]
        dst[half:half + CHUNK, :] = m_ref[...]
        dst[half + CHUNK:, :] = n_ref[...]

    is_first = first_ref[c] != 0
    is_last = last_ref[c] != 0
    lane = lax.broadcasted_iota(jnp.int32, (tq, LANES), 1)
    lo = lane < HEAD_DIM
    row2 = lax.broadcasted_iota(jnp.int32, (2 * tq, 1), 0)

    def sub_block(sb):
        r0 = pl.multiple_of(sb * tq, tq)
        variant = ((is_first & (sb == 0)).astype(jnp.int32)
                   + 2 * (is_last & (sb == n_sub - 1)).astype(jnp.int32))
        kwin = kbuf[pl.ds(r0, win), :]
        vwin = vbuf[pl.ds(r0, win), :]
        for p in range(PAIRS_PER_GROUP):
            ql = slice(p * LANES, (p + 1) * LANES)
            q2 = q_ref[pl.ds(r0, tq), ql] * jnp.asarray(HEAD_DIM ** -0.5, _BF16)
            s = _pair_scores(q2, kwin, bias_ref[p, variant], lo)
            sink = jnp.where(row2 < tq, sink_ref[hh, 2 * p], sink_ref[hh, 2 * p + 1])
            m_new = jnp.maximum(jnp.max(s, axis=-1, keepdims=True), sink)
            pr = jnp.exp(s - m_new)
            l_new = jnp.exp(sink - m_new) + jnp.sum(pr, axis=-1, keepdims=True)
            pv = _by_head(jnp.dot(pr.astype(_BF16), vwin, preferred_element_type=_F32), lo)
            o_ref[pl.ds(r0, tq), ql] = (pv * _by_head(1.0 / l_new, lo)).astype(o_ref.dtype)

    def body(it, carry):
        for u in range(SWA_UNROLL):
            sub_block(it * SWA_UNROLL + u)
        return carry

    lax.fori_loop(0, n_sub // SWA_UNROLL, body, 0)


def _swa_attention(proj, slopes, sink, flags):
    t = proj.shape[0]
    nch = t // CHUNK
    per = CHUNK // SWA_HALF
    win = ATT_TQ + 2 * SWA_HALF
    q0 = 3 * A_WIDTH // HEAD_GROUP
    k0 = (3 * A_WIDTH + B_WIDTH) // LANES
    v0 = k0 + B_KV_WIDTH // LANES

    def main(col0):
        return pl.BlockSpec((CHUNK, LANES), lambda hh, c, *_: (c, col0 + hh))

    def prev(col0):
        return pl.BlockSpec((SWA_HALF, LANES),
                            lambda hh, c, *_: (jnp.maximum(c * per - 1, 0), col0 + hh))

    def nxt(col0):
        return pl.BlockSpec((SWA_HALF, LANES),
                            lambda hh, c, *_: (jnp.minimum((c + 1) * per, nch * per - 1), col0 + hh))

    smem = pl.BlockSpec(memory_space=pltpu.SMEM)
    return pl.pallas_call(
        _swa_kernel,
        out_shape=jax.ShapeDtypeStruct((t, B_WIDTH), _BF16),
        grid_spec=pltpu.PrefetchScalarGridSpec(
            num_scalar_prefetch=2,
            grid=(B_WIDTH // HEAD_GROUP, nch),
            in_specs=[smem, smem,
                      pl.BlockSpec((CHUNK, HEAD_GROUP), lambda hh, c, *_: (c, q0 + hh)),
                      prev(k0), main(k0), nxt(k0), prev(v0), main(v0), nxt(v0)],
            out_specs=pl.BlockSpec((CHUNK, HEAD_GROUP), lambda hh, c, *_: (c, hh)),
            scratch_shapes=[pltpu.VMEM((CHUNK + 2 * SWA_HALF, LANES), _BF16),
                            pltpu.VMEM((CHUNK + 2 * SWA_HALF, LANES), _BF16),
                            pltpu.VMEM((PAIRS_PER_GROUP, 4, 2 * ATT_TQ, win), _F32)],
        ),
        compiler_params=_cparams(2),
        name="swa_attn",
    )(*flags, slopes, sink, *([proj] * 7))


def _outproj_kernel(*refs, moe):
    (oa_ref, ob_ref, ga_ref, gb_ref, w_ref, x_ref, g1_ref, sh2_ref, sc2_ref,
     lng_ref, lnb_ref) = refs[:11]
    if moe:
        wr_ref, x1_ref, h2_ref, rt_ref = refs[11:]
    else:
        x1_ref, h2_ref = refs[11:]

    def rms(o_ref, g_ref):
        o = o_ref[...].astype(_F32)
        return o * lax.rsqrt(jnp.mean(o * o, axis=-1, keepdims=True) + LN_EPS) * g_ref[...]

    u = jnp.concatenate([rms(oa_ref, ga_ref), rms(ob_ref, gb_ref)], axis=-1).astype(_BF16)
    y = jnp.dot(u, w_ref[...], preferred_element_type=_F32)
    x1 = _layer_norm(DEEPNORM_ALPHA * x_ref[...] + g1_ref[0] * y, lng_ref[...], lnb_ref[...])
    x1_ref[...] = x1
    h2 = x1 * (1.0 + sc2_ref[0]) + sh2_ref[0]
    h2_ref[...] = h2.astype(h2_ref.dtype)
    if moe:
        logits = jnp.dot(h2.astype(_BF16), wr_ref[...], preferred_element_type=_F32)
        lane = lax.broadcasted_iota(jnp.int32, logits.shape, 1)
        valid = lane < N_EXPERTS
        logits = jnp.where(valid, logits, NEG_INF)
        e = jnp.exp(logits - jnp.max(logits, axis=-1, keepdims=True))
        probs = jnp.where(valid, e / jnp.sum(e, axis=-1, keepdims=True), -1.0)
        p1 = jnp.max(probs, axis=-1, keepdims=True)
        i1 = jnp.min(jnp.where(probs == p1, lane, LANES), axis=-1, keepdims=True)
        rest = jnp.where(lane == i1, -1.0, probs)
        p2 = jnp.max(rest, axis=-1, keepdims=True)
        i2 = jnp.min(jnp.where(rest == p2, lane, LANES), axis=-1, keepdims=True)
        den = p1 + p2
        rt = jnp.where(lane == 0, i1.astype(_F32), 0.0)
        rt = jnp.where(lane == 1, i2.astype(_F32), rt)
        rt = jnp.where(lane == 2, p1 / den, rt)
        rt = jnp.where(lane == 3, p2 / den, rt)
        rt_ref[...] = rt


def _outproj(oa, ob, g_a, g_b, w_out, x, mod_l, ln_g, ln_b, w_router=None, tm=512):
    t = x.shape[0]
    moe = w_router is not None
    in_specs = [_row_spec(tm, A_WIDTH), _row_spec(tm, B_WIDTH),
                _const_spec((1, A_WIDTH)), _const_spec((1, B_WIDTH)),
                _const_spec((A_WIDTH + B_WIDTH, D_MODEL)), _row_spec(tm, D_MODEL),
                _mod_spec(tm, 2), _mod_spec(tm, 3), _mod_spec(tm, 4),
                _const_spec((1, D_MODEL)), _const_spec((1, D_MODEL))]
    args = [oa, ob, g_a, g_b, w_out, x, mod_l, mod_l, mod_l, ln_g, ln_b]
    out_shape = [jax.ShapeDtypeStruct((t, D_MODEL), _F32),
                 jax.ShapeDtypeStruct((t, D_MODEL), _F32 if moe else _BF16)]
    out_specs = [_row_spec(tm, D_MODEL), _row_spec(tm, D_MODEL)]
    if moe:
        in_specs.append(_const_spec((D_MODEL, LANES)))
        args.append(w_router)
        out_shape.append(jax.ShapeDtypeStruct((t, LANES), _F32))
        out_specs.append(_row_spec(tm, LANES))
    return pl.pallas_call(
        functools.partial(_outproj_kernel, moe=moe),
        out_shape=tuple(out_shape),
        grid=(t // tm,),
        in_specs=in_specs,
        out_specs=tuple(out_specs),
        compiler_params=_cparams(1),
        name="out_proj_ln",
    )(*args)


def _ffn_kernel(*refs, n_f, epilogue, emit_h):
    te_ref, tv_ref, h_ref, wg_ref, wu_ref, wd_ref = refs[:6]
    rest = refs[6:]
    if epilogue:
        x_ref, g2_ref, lng_ref, lnb_ref = rest[:4]
        rest = rest[4:]
        if emit_h:
            sh_ref, sc_ref = rest[:2]
            rest = rest[2:]
    y_ref = rest[0]
    if emit_h:
        hn_ref = rest[1]
    i = pl.program_id(0)
    j = pl.program_id(1)
    valid = tv_ref[i] != 0

    def part():
        h = h_ref[...]
        g = jnp.dot(h, wg_ref[0], preferred_element_type=_F32)
        u = jnp.dot(h, wu_ref[0], preferred_element_type=_F32)
        a = (g * jax.nn.sigmoid(g) * u).astype(_BF16)
        return jnp.dot(a, wd_ref[0], preferred_element_type=_F32)

    @pl.when(jnp.logical_not(valid) & (j == 0))
    def _():
        y_ref[...] = jnp.zeros_like(y_ref)

    @pl.when(valid & (j == 0))
    def _():
        y_ref[...] = part()

    @pl.when(valid & (j > 0) & (j < n_f - 1))
    def _():
        y_ref[...] += part()

    @pl.when(valid & (j == n_f - 1))
    def _():
        y = y_ref[...] + part()
        if epilogue:
            y = _layer_norm(DEEPNORM_ALPHA * x_ref[...] + g2_ref[0] * y, lng_ref[...], lnb_ref[...])
            if emit_h:
                hn_ref[...] = (y * (1.0 + sc_ref[0]) + sh_ref[0]).astype(hn_ref.dtype)
        y_ref[...] = y


def _ffn(h, w_gate, w_up, w_down, tile_expert, tile_valid, *, tm, tf=512, ln=None):
    rows = h.shape[0]
    f = w_gate.shape[2]
    n_f = f // tf
    assert n_f >= 2
    in_specs = [pl.BlockSpec((tm, D_MODEL), lambda i, j, te, tv: (i, 0)),
                pl.BlockSpec((1, D_MODEL, tf), lambda i, j, te, tv: (te[i], 0, j)),
                pl.BlockSpec((1, D_MODEL, tf), lambda i, j, te, tv: (te[i], 0, j)),
                pl.BlockSpec((1, tf, D_MODEL), lambda i, j, te, tv: (te[i], j, 0))]
    args = [h, w_gate, w_up, w_down]
    out_shape = [jax.ShapeDtypeStruct((rows, D_MODEL), _F32)]
    out_specs = [_row_spec(tm, D_MODEL)]
    emit_h = False
    if ln is not None:
        x, mod_l, ln_g, ln_b, mod_next = ln
        in_specs += [_row_spec(tm, D_MODEL), _mod_spec(tm, 5),
                     _const_spec((1, D_MODEL)), _const_spec((1, D_MODEL))]
        args += [x, mod_l, ln_g, ln_b]
        emit_h = mod_next is not None
        if emit_h:
            in_specs += [_mod_spec(tm, 0), _mod_spec(tm, 1)]
            args += [mod_next, mod_next]
            out_shape.append(jax.ShapeDtypeStruct((rows, D_MODEL), _BF16))
            out_specs.append(_row_spec(tm, D_MODEL))
    out = pl.pallas_call(
        functools.partial(_ffn_kernel, n_f=n_f, epilogue=ln is not None, emit_h=emit_h),
        out_shape=tuple(out_shape),
        grid_spec=pltpu.PrefetchScalarGridSpec(
            num_scalar_prefetch=2,
            grid=(rows // tm, n_f),
            in_specs=in_specs,
            out_specs=tuple(out_specs),
        ),
        compiler_params=_cparams(2),
        name="ffn_fused",
    )(tile_expert, tile_valid, *args)
    return out if emit_h else (out[0], None)


def _start_row_copies(idx_ref, base, src_hbm, buf, sem, tm):
    def body(kk, c):
        for u in range(DMA_UNROLL):
            k = kk * DMA_UNROLL + u
            pltpu.make_async_copy(src_hbm.at[pl.ds(idx_ref[base + k], 1), :],
                                  buf.at[pl.ds(k, 1), :], sem).start()
        return c
    lax.fori_loop(0, tm // DMA_UNROLL, body, 0)


def _wait_row_copies(src_hbm, buf, sem, tm):
    pltpu.make_async_copy(src_hbm.at[pl.ds(0, tm), :], buf, sem).wait()


def _gather_kernel(src_ref, tv_ref, h_hbm, o_ref, buf, sem, *, tm):
    i = pl.program_id(0)

    @pl.when(tv_ref[i] != 0)
    def _():
        _start_row_copies(src_ref, i * tm, h_hbm, buf, sem, tm)
        _wait_row_copies(h_hbm, buf, sem, tm)
        o_ref[...] = buf[...].astype(o_ref.dtype)

    @pl.when(tv_ref[i] == 0)
    def _():
        o_ref[...] = jnp.zeros_like(o_ref)


def _gather_rows(h, src_token, tile_valid, tm):
    rows = src_token.shape[0]
    return pl.pallas_call(
        functools.partial(_gather_kernel, tm=tm),
        out_shape=jax.ShapeDtypeStruct((rows, D_MODEL), _BF16),
        grid_spec=pltpu.PrefetchScalarGridSpec(
            num_scalar_prefetch=2,
            grid=(rows // tm,),
            in_specs=[pl.BlockSpec(memory_space=pl.ANY)],
            out_specs=pl.BlockSpec((tm, D_MODEL), lambda i, src, tv: (i, 0)),
            scratch_shapes=[pltpu.VMEM((tm, D_MODEL), _F32), pltpu.SemaphoreType.DMA],
        ),
        compiler_params=_cparams(1),
        name="moe_gather",
    )(src_token, tile_valid, h)


def _combine_kernel(pa_ref, pb_ref, ys_hbm, rt_ref, x_ref, g2_ref, lng_ref, lnb_ref,
                    o_ref, buf_a, buf_b, sem_a, sem_b, *, tm):
    i = pl.program_id(0)
    _start_row_copies(pa_ref, i * tm, ys_hbm, buf_a, sem_a, tm)
    _start_row_copies(pb_ref, i * tm, ys_hbm, buf_b, sem_b, tm)
    _wait_row_copies(ys_hbm, buf_a, sem_a, tm)
    _wait_row_copies(ys_hbm, buf_b, sem_b, tm)
    rt = rt_ref[...]
    y = rt[:, 2:3] * buf_a[...] + rt[:, 3:4] * buf_b[...]
    o_ref[...] = _layer_norm(DEEPNORM_ALPHA * x_ref[...] + g2_ref[0] * y,
                             lng_ref[...], lnb_ref[...])


def _combine_ln(ys, pos_a, pos_b, rt, x, mod_l, ln_g, ln_b, tm=256):
    t = x.shape[0]
    return pl.pallas_call(
        functools.partial(_combine_kernel, tm=tm),
        out_shape=jax.ShapeDtypeStruct((t, D_MODEL), _F32),
        grid_spec=pltpu.PrefetchScalarGridSpec(
            num_scalar_prefetch=2,
            grid=(t // tm,),
            in_specs=[pl.BlockSpec(memory_space=pl.ANY), _row_spec(tm, LANES),
                      _row_spec(tm, D_MODEL), _mod_spec(tm, 5),
                      _const_spec((1, D_MODEL)), _const_spec((1, D_MODEL))],
            out_specs=_row_spec(tm, D_MODEL),
            scratch_shapes=[pltpu.VMEM((tm, D_MODEL), _F32), pltpu.VMEM((tm, D_MODEL), _F32),
                            pltpu.SemaphoreType.DMA, pltpu.SemaphoreType.DMA],
        ),
        compiler_params=_cparams(1),
        name="moe_combine_ln",
    )(pos_a, pos_b, ys, rt, x, mod_l, ln_g, ln_b)


def _route(rt, tm):
    t = rt.shape[0]
    n_tiles = (2 * t) // tm + N_EXPERTS
    experts = rt[:, 0:2].astype(jnp.int32).reshape(-1)
    onehot = (experts[:, None] == jnp.arange(N_EXPERTS, dtype=jnp.int32)[None, :])
    csum = jnp.cumsum(onehot.astype(jnp.int32), axis=0)
    rank = jnp.take_along_axis(csum, experts[:, None], axis=1)[:, 0] - 1
    counts = csum[-1]
    tiles_per = (counts + tm - 1) // tm
    tile_end = jnp.cumsum(tiles_per)
    tile_start = tile_end - tiles_per
    pos = tile_start[experts] * tm + rank
    src_token = jnp.zeros((n_tiles * tm,), jnp.int32).at[pos].set(
        jnp.arange(2 * t, dtype=jnp.int32) // 2)
    tile_ids = jnp.arange(n_tiles, dtype=jnp.int32)
    tile_expert = jnp.minimum(
        jnp.sum((tile_ids[:, None] >= tile_end[None, :]).astype(jnp.int32), axis=1),
        N_EXPERTS - 1).astype(jnp.int32)
    tile_valid = (tile_ids < tile_end[-1]).astype(jnp.int32)
    last_expert = tile_expert[jnp.maximum(tile_end[-1] - 1, 0)]
    tile_expert = jnp.where(tile_valid != 0, tile_expert, last_expert)
    pos2 = pos.reshape(t, 2)
    return src_token, pos2[:, 0], pos2[:, 1], tile_expert, tile_valid


def _head_orders():
    perm = np.array([8 * hh + p + 4 * s for hh in range(2) for p in range(4) for s in range(2)])
    feat = (perm[:, None] * HEAD_DIM + np.arange(HEAD_DIM)[None, :]).reshape(-1)
    return perm, feat


def kernel(x_prompt, x_sample, c_prompt, c_sample, w_ada, b_ada, w_in, w_out, g_out, attn_sink,
           ln1_g, ln1_b, ln2_g, ln2_b, ffn_w_gate, ffn_w_up, ffn_w_down,
           moe_router, moe_w_gate, moe_w_up, moe_w_down):
    nb_p, s_p, _ = x_prompt.shape
    nb_s, s_s, _ = x_sample.shape
    t_p, t_s = nb_p * s_p, nb_s * s_s
    t = t_p + t_s
    groups = ((0, nb_p, s_p), (t_p, nb_s, s_s))
    for w, r in DILATED_PAIRS:
        assert w // (2 * r) == DIL_HALF and CHUNK % (r * ATT_TQ) == 0
    assert s_p % CHUNK == 0 and s_s % CHUNK == 0 and CHUNK % DIL_HALO == 0

    x = jnp.concatenate([x_prompt.reshape(t_p, D_MODEL), x_sample.reshape(t_s, D_MODEL)], axis=0)
    c_chunks = jnp.concatenate([jnp.repeat(c_prompt, s_p // CHUNK, axis=0),
                                jnp.repeat(c_sample, s_s // CHUNK, axis=0)], axis=0)
    nch = c_chunks.shape[0]
    mod = _ada(c_chunks, w_ada, b_ada).reshape(DEPTH, nch, 1, 6 * D_MODEL)
    flags = _chunk_flags(groups)

    perm_b, feat_b = _head_orders()
    slopes = 2.0 ** (-8.0 * np.arange(1, N_HEADS + 1) / N_HEADS)
    slopes_a = jnp.asarray(slopes.reshape(N_HEADS // 2, 2), _F32)
    slopes_b = jnp.asarray(slopes[perm_b].reshape(2, 8), _F32)
    qb0 = 3 * A_WIDTH
    in_cols = np.concatenate([np.arange(qb0), qb0 + feat_b, np.arange(qb0 + B_WIDTH, IN_WIDTH)])
    out_rows = np.concatenate([np.arange(A_WIDTH), A_WIDTH + feat_b])

    one_tile = jnp.zeros((t // FFN_TM,), jnp.int32), jnp.ones((t // FFN_TM,), jnp.int32)

    h = _modulate(x, mod[0])
    for l in range(DEPTH):
        mod_l = mod[l]
        w_in_l = w_in[l][:, in_cols].astype(_BF16)
        w_out_l = w_out[l][out_rows, :].astype(_BF16)
        g_l = g_out[l][out_rows]
        g_a, g_b = g_l[:A_WIDTH].reshape(1, A_WIDTH), g_l[A_WIDTH:].reshape(1, B_WIDTH)
        sink_b = attn_sink[l][perm_b].reshape(2, 8).astype(_F32)

        proj = _matmul(h, w_in_l)
        oa = _dilated_attention(proj, slopes_a, flags)
        ob = _swa_attention(proj, slopes_b, sink_b, flags)

        ln1 = ln1_g[l].reshape(1, D_MODEL), ln1_b[l].reshape(1, D_MODEL)
        ln2 = ln2_g[l].reshape(1, D_MODEL), ln2_b[l].reshape(1, D_MODEL)
        e = l // 2
        if l % 2 == 0:
            x, h2 = _outproj(oa, ob, g_a, g_b, w_out_l, x, mod_l, *ln1)
            x, h = _ffn(h2, ffn_w_gate[e:e + 1].astype(_BF16), ffn_w_up[e:e + 1].astype(_BF16),
                        ffn_w_down[e:e + 1].astype(_BF16), *one_tile, tm=FFN_TM,
                        ln=(x, mod_l, *ln2, mod[l + 1] if l + 1 < DEPTH else None))
        else:
            w_r = jnp.pad(moe_router[e], ((0, 0), (0, LANES - N_EXPERTS))).astype(_BF16)
            x, h2, rt = _outproj(oa, ob, g_a, g_b, w_out_l, x, mod_l, *ln1, w_router=w_r)
            src_token, pos_a, pos_b, tile_expert, tile_valid = _route(rt, MOE_TM)
            xs = _gather_rows(h2, src_token, tile_valid, MOE_TM)
            ys, _ = _ffn(xs, moe_w_gate[e].astype(_BF16), moe_w_up[e].astype(_BF16),
                         moe_w_down[e].astype(_BF16), tile_expert, tile_valid, tm=MOE_TM)
            x = _combine_ln(ys, pos_a, pos_b, rt, x, mod_l, *ln2)
            if l + 1 < DEPTH:
                h = _modulate(x, mod[l + 1])

    y_prompt = x[:t_p].reshape(nb_p, s_p, D_MODEL)
    y_sample = x[t_p:].reshape(nb_s, s_s, D_MODEL)
    return y_prompt, y_sample
```

```python
import functools

import jax
import jax.numpy as jnp
import numpy as np
from jax import lax
from jax.experimental import pallas as pl
from jax.experimental.pallas import tpu as pltpu

D_MODEL = 2048
DEPTH = 2
HEAD_DIM = 64
N_HEADS = 16
B_KV_HEADS = 4
A_WIDTH = N_HEADS * HEAD_DIM
B_WIDTH = N_HEADS * HEAD_DIM
B_KV_WIDTH = B_KV_HEADS * HEAD_DIM
IN_WIDTH = 3 * A_WIDTH + B_WIDTH + 2 * B_KV_WIDTH
DILATED_PAIRS = ((128, 1), (512, 4), (2048, 16))
DILATIONS = tuple(sorted((r for _, r in DILATED_PAIRS), reverse=True))
assert DILATIONS[-1] == 1
DIL_HALF = DILATED_PAIRS[0][0] // 2
DIL_HALO = DIL_HALF * max(DILATIONS)
SWA_HALF = 128
N_EXPERTS = 8
DEEPNORM_ALPHA = (2.0 * DEPTH) ** 0.25
LN_EPS = 1e-5
NEG_INF = -1e30

CHUNK = 2048
LANES = 128
HEAD_GROUP = 512
PAIRS_PER_GROUP = HEAD_GROUP // LANES
ATT_TQ = 128
DIL_UNROLL = 8
SWA_UNROLL = 2
MOE_TM = 512
FFN_TM = 512
DMA_UNROLL = 8
VMEM_LIMIT = 56 * 1024 * 1024

_BF16 = jnp.bfloat16
_F32 = jnp.float32


def _cparams(n_axes):
    return pltpu.CompilerParams(dimension_semantics=("arbitrary",) * n_axes,
                                vmem_limit_bytes=VMEM_LIMIT)


def _layer_norm(z, g, b):
    mu = jnp.mean(z, axis=-1, keepdims=True)
    zc = z - mu
    var = jnp.mean(zc * zc, axis=-1, keepdims=True)
    return zc * lax.rsqrt(var + LN_EPS) * g + b


def _ada_kernel(c_ref, w_ref, b_ref, o_ref):
    c = c_ref[...]
    sc = (c * jax.nn.sigmoid(c)).astype(_BF16)
    w = w_ref[0].astype(_BF16)
    o_ref[0] = jnp.dot(sc, w, preferred_element_type=_F32) + b_ref[0]


def _ada(c_chunks, w_ada, b_ada):
    nch = c_chunks.shape[0]
    tn = 1024
    n_out = w_ada.shape[2]
    return pl.pallas_call(
        _ada_kernel,
        out_shape=jax.ShapeDtypeStruct((DEPTH, nch, n_out), _F32),
        grid=(DEPTH, n_out // tn),
        in_specs=[
            pl.BlockSpec((nch, D_MODEL), lambda l, j: (0, 0)),
            pl.BlockSpec((1, D_MODEL, tn), lambda l, j: (l, 0, j)),
            pl.BlockSpec((1, 1, tn), lambda l, j: (l, 0, j)),
        ],
        out_specs=pl.BlockSpec((1, nch, tn), lambda l, j: (l, 0, j)),
        compiler_params=_cparams(2),
        name="ada_mod",
    )(c_chunks, w_ada, b_ada.reshape(DEPTH, 1, n_out))


def _part_specs(parts, tm):
    specs, start = [], 0
    for p in parts:
        n = p.shape[0] // tm
        specs.append(pl.BlockSpec(
            (tm, p.shape[1]),
            lambda i, *_, start=start, n=n: (jnp.clip(i - start, 0, n - 1), 0)))
        start += n
    return specs


def _read_parts(refs, parts_tiles, i):
    x = refs[0][...]
    bound = 0
    for k in range(1, len(refs)):
        bound += parts_tiles[k - 1]
        x = jnp.where(i >= bound, refs[k][...], x)
    return x


def _modulate_kernel(*refs, parts_tiles):
    n = len(parts_tiles)
    sh_ref, sc_ref, o_ref = refs[n:]
    x = _read_parts(refs[:n], parts_tiles, pl.program_id(0))
    o_ref[...] = (x * (1.0 + sc_ref[0]) + sh_ref[0]).astype(o_ref.dtype)


def _mod_spec(tm, which):
    assert CHUNK % tm == 0
    per_chunk = CHUNK // tm
    return pl.BlockSpec((1, 1, D_MODEL), lambda i, *_: (i // per_chunk, 0, which))


def _row_spec(tm, width):
    return pl.BlockSpec((tm, width), lambda i, *_: (i, 0))


def _const_spec(shape):
    return pl.BlockSpec(shape, lambda *_: (0,) * len(shape))


def _modulate(x_parts, mod_l, tm=1024):
    t = sum(p.shape[0] for p in x_parts)
    parts_tiles = tuple(p.shape[0] // tm for p in x_parts)
    return pl.pallas_call(
        functools.partial(_modulate_kernel, parts_tiles=parts_tiles),
        out_shape=jax.ShapeDtypeStruct((t, D_MODEL), _BF16),
        grid=(t // tm,),
        in_specs=_part_specs(x_parts, tm) + [_mod_spec(tm, 0), _mod_spec(tm, 1)],
        out_specs=_row_spec(tm, D_MODEL),
        compiler_params=_cparams(1),
        name="modulate",
    )(*x_parts, mod_l, mod_l)


def _matmul_kernel(x_ref, w_ref, o_ref):
    o_ref[...] = jnp.dot(x_ref[...], w_ref[...],
                         preferred_element_type=_F32).astype(o_ref.dtype)


def _matmul(x, w, tm=1024, tn=1536):
    t, k = x.shape
    n = w.shape[1]
    return pl.pallas_call(
        _matmul_kernel,
        out_shape=jax.ShapeDtypeStruct((t, n), _BF16),
        grid=(n // tn, t // tm),
        in_specs=[pl.BlockSpec((tm, k), lambda j, i: (i, 0)),
                  pl.BlockSpec((k, tn), lambda j, i: (0, j))],
        out_specs=pl.BlockSpec((tm, tn), lambda j, i: (i, j)),
        compiler_params=_cparams(2),
        name="in_proj",
    )(x, w)


def _bias_tables(bias_ref, lead, slopes, half, dist_scale):
    tq = ATT_TQ
    win = tq + 2 * half
    row = lax.broadcasted_iota(jnp.int32, (tq, win), 0)
    col = lax.broadcasted_iota(jnp.int32, (tq, win), 1)
    dist = jnp.abs(col - half - row)
    scaled = dist.astype(_F32) * dist_scale
    band = dist <= half
    for variant in range(4):
        ok = band
        if variant & 1:
            ok = ok & (col >= half)
        if variant & 2:
            ok = ok & (col < tq + half)
        for s in range(2):
            bias_ref[lead + (variant, slice(s * tq, (s + 1) * tq), slice(None))] = jnp.where(
                ok, -(slopes[s] * scaled), NEG_INF)


def _pair_scores(q2, kwin, bias, lo):
    zero = jnp.zeros_like(q2)
    qq = jnp.concatenate([jnp.where(lo, q2, zero), jnp.where(lo, zero, q2)], axis=0)
    s = lax.dot_general(qq.astype(_BF16), kwin, (((1,), (1,)), ((), ())),
                        preferred_element_type=_F32)
    return s + bias


def _by_head(col, lo):
    tq = col.shape[0] // 2
    return jnp.where(lo, col[:tq], col[tq:])


def _chunk_flags(groups):
    first, last = [], []
    for _, n_seq, seq in groups:
        per = seq // CHUNK
        for _ in range(n_seq):
            first += [1] + [0] * (per - 1)
            last += [0] * (per - 1) + [1]
    return jnp.asarray(first, jnp.int32), jnp.asarray(last, jnp.int32)


def _dilated_kernel(first_ref, last_ref, slope_ref, q_ref, kp_ref, km_ref, kn_ref,
                    vp_ref, vm_ref, vn_ref, o_ref, qf, kf, vf, acc, mst, lst, bias_ref):
    tq = ATT_TQ
    half = DIL_HALF
    win = tq + 2 * half
    pair = pl.program_id(0)
    c = pl.program_id(1)

    @pl.when(c == 0)
    def _():
        slopes = (slope_ref[pair, 0], slope_ref[pair, 1])
        for b, r in enumerate(DILATIONS):
            _bias_tables(bias_ref, (b,), slopes, half, float(r))

    qf[...] = q_ref[...].astype(_F32) * (HEAD_DIM ** -0.5)
    for dst, (p_ref, m_ref, n_ref) in ((kf, (kp_ref, km_ref, kn_ref)), (vf, (vp_ref, vm_ref, vn_ref))):
        dst[0:DIL_HALO, :] = p_ref[...].astype(_F32)
        dst[DIL_HALO:DIL_HALO + CHUNK, :] = m_ref[...].astype(_F32)
        dst[DIL_HALO + CHUNK:, :] = n_ref[...].astype(_F32)

    is_first = first_ref[c] != 0
    is_last = last_ref[c] != 0
    lane = lax.broadcasted_iota(jnp.int32, (tq, LANES), 1)
    lo = lane < HEAD_DIM

    def rows(start, size, r):
        if r == 1:
            return pl.ds(pl.multiple_of(start, half), size)
        return pl.ds(start, size, stride=r)

    def local_softmax(b, r, rho, sb, n_sub):
        i0 = sb * tq
        qrows = rows(rho + r * i0, tq, r)
        krows = rows(DIL_HALO + rho + r * (i0 - half), win, r)
        variant = ((is_first & (sb == 0)).astype(jnp.int32)
                   + 2 * (is_last & (sb == n_sub - 1)).astype(jnp.int32))
        s = _pair_scores(qf[qrows, :], kf[krows, :].astype(_BF16), bias_ref[b, variant], lo)
        m_cur = jnp.max(s, axis=-1, keepdims=True)
        pr = jnp.exp(s - m_cur)
        l_cur = jnp.sum(pr, axis=-1, keepdims=True)
        pv = _by_head(jnp.dot(pr.astype(_BF16), vf[krows, :].astype(_BF16),
                              preferred_element_type=_F32), lo)
        return qrows, pv, _by_head(m_cur, lo), _by_head(l_cur, lo)

    def merge(b, qrows, pv, m_h, l_h):
        if b == 0:
            acc[qrows, :] = pv
            mst[qrows, :] = m_h
            lst[qrows, :] = l_h
            return
        m_p = mst[qrows, :]
        m_n = jnp.maximum(m_p, m_h)
        a_p = jnp.exp(m_p - m_n)
        a_c = jnp.exp(m_h - m_n)
        acc_n = acc[qrows, :] * a_p + pv * a_c
        l_n = lst[qrows, :] * a_p + l_h * a_c
        if b == len(DILATIONS) - 1:
            o_ref[qrows, :] = (acc_n / l_n).astype(o_ref.dtype)
        else:
            acc[qrows, :] = acc_n
            mst[qrows, :] = m_n
            lst[qrows, :] = l_n

    for b, r in enumerate(DILATIONS):
        n_sub = CHUNK // (r * tq)

        def body(it, carry, b=b, r=r, n_sub=n_sub):
            parts = []
            for u in range(DIL_UNROLL):
                idx = it * DIL_UNROLL + u
                parts.append(local_softmax(b, r, idx // n_sub, idx % n_sub, n_sub))
            for part in parts:
                merge(b, *part)
            return carry
        lax.fori_loop(0, r * n_sub // DIL_UNROLL, body, 0)


def _dilated_attention(proj, slopes, flags):
    t = proj.shape[0]
    nch = t // CHUNK
    per = CHUNK // DIL_HALO
    n_pairs = A_WIDTH // LANES
    win = ATT_TQ + 2 * DIL_HALF

    def main(col0):
        return pl.BlockSpec((CHUNK, LANES), lambda p, c, *_: (c, col0 + p))

    def prev(col0):
        return pl.BlockSpec((DIL_HALO, LANES),
                            lambda p, c, *_: (jnp.maximum(c * per - 1, 0), col0 + p))

    def nxt(col0):
        return pl.BlockSpec((DIL_HALO, LANES),
                            lambda p, c, *_: (jnp.minimum((c + 1) * per, nch * per - 1), col0 + p))

    k0, v0 = A_WIDTH // LANES, 2 * A_WIDTH // LANES
    return pl.pallas_call(
        _dilated_kernel,
        out_shape=jax.ShapeDtypeStruct((t, A_WIDTH), _BF16),
        grid_spec=pltpu.PrefetchScalarGridSpec(
            num_scalar_prefetch=2,
            grid=(n_pairs, nch),
            in_specs=[pl.BlockSpec(memory_space=pltpu.SMEM), main(0),
                      prev(k0), main(k0), nxt(k0), prev(v0), main(v0), nxt(v0)],
            out_specs=main(0),
            scratch_shapes=[pltpu.VMEM((CHUNK, LANES), _F32),
                            pltpu.VMEM((CHUNK + 2 * DIL_HALO, LANES), _F32),
                            pltpu.VMEM((CHUNK + 2 * DIL_HALO, LANES), _F32),
                            pltpu.VMEM((CHUNK, LANES), _F32),
                            pltpu.VMEM((CHUNK, LANES), _F32),
                            pltpu.VMEM((CHUNK, LANES), _F32),
                            pltpu.VMEM((len(DILATIONS), 4, 2 * ATT_TQ, win), _F32)],
        ),
        compiler_params=_cparams(2),
        name="dilated_attn",
    )(*flags, slopes, *([proj] * 7))


def _swa_kernel(first_ref, last_ref, slope_ref, sink_ref, q_ref, kp_ref, km_ref, kn_ref,
                vp_ref, vm_ref, vn_ref, o_ref, kbuf, vbuf, bias_ref):
    tq = ATT_TQ
    half = SWA_HALF
    win = tq + 2 * half
    n_sub = CHUNK // tq
    hh = pl.program_id(0)
    c = pl.program_id(1)

    @pl.when(c == 0)
    def _():
        for p in range(PAIRS_PER_GROUP):
            _bias_tables(bias_ref, (p,), (slope_ref[hh, 2 * p], slope_ref[hh, 2 * p + 1]),
                         half, 1.0)

    for dst, (p_ref, m_ref, n_ref) in ((kbuf, (kp_ref, km_ref, kn_ref)),
                                       (vbuf, (vp_ref, vm_ref, vn_ref))):
        dst[0:half, :] = p_ref[...]
        dst[half:half + CHUNK, :] = m_ref[...]
        dst[half + CHUNK:, :] = n_ref[...]

    is_first = first_ref[c] != 0
    is_last = last_ref[c] != 0
    lane = lax.broadcasted_iota(jnp.int32, (tq, LANES), 1)
    lo = lane < HEAD_DIM
    row2 = lax.broadcasted_iota(jnp.int32, (2 * tq, 1), 0)

    def sub_block(sb):
        r0 = pl.multiple_of(sb * tq, tq)
        variant = ((is_first & (sb == 0)).astype(jnp.int32)
                   + 2 * (is_last & (sb == n_sub - 1)).astype(jnp.int32))
        kwin = kbuf[pl.ds(r0, win), :]
        vwin = vbuf[pl.ds(r0, win), :]
        for p in range(PAIRS_PER_GROUP):
            ql = slice(p * LANES, (p + 1) * LANES)
            q2 = q_ref[pl.ds(r0, tq), ql] * jnp.asarray(HEAD_DIM ** -0.5, _BF16)
            s = _pair_scores(q2, kwin, bias_ref[p, variant], lo)
            sink = jnp.where(row2 < tq, sink_ref[hh, 2 * p], sink_ref[hh, 2 * p + 1])
            m_new = jnp.maximum(jnp.max(s, axis=-1, keepdims=True), sink)
            pr = jnp.exp(s - m_new)
            l_new = jnp.exp(sink - m_new) + jnp.sum(pr, axis=-1, keepdims=True)
            pv = _by_head(jnp.dot(pr.astype(_BF16), vwin, preferred_element_type=_F32), lo)
            o_ref[pl.ds(r0, tq), ql] = (pv * _by_head(1.0 / l_new, lo)).astype(o_ref.dtype)

    def body(it, carry):
        for u in range(SWA_UNROLL):
            sub_block(it * SWA_UNROLL + u)
        return carry

    lax.fori_loop(0, n_sub // SWA_UNROLL, body, 0)


def _swa_attention(proj, slopes, sink, flags):
    t = proj.shape[0]
    nch = t // CHUNK
    per = CHUNK // SWA_HALF
    win = ATT_TQ + 2 * SWA_HALF
    q0 = 3 * A_WIDTH // HEAD_GROUP
    k0 = (3 * A_WIDTH + B_WIDTH) // LANES
    v0 = k0 + B_KV_WIDTH // LANES

    def main(col0):
        return pl.BlockSpec((CHUNK, LANES), lambda hh, c, *_: (c, col0 + hh))

    def prev(col0):
        return pl.BlockSpec((SWA_HALF, LANES),
                            lambda hh, c, *_: (jnp.maximum(c * per - 1, 0), col0 + hh))

    def nxt(col0):
        return pl.BlockSpec((SWA_HALF, LANES),
                            lambda hh, c, *_: (jnp.minimum((c + 1) * per, nch * per - 1), col0 + hh))

    smem = pl.BlockSpec(memory_space=pltpu.SMEM)
    return pl.pallas_call(
        _swa_kernel,
        out_shape=jax.ShapeDtypeStruct((t, B_WIDTH), _BF16),
        grid_spec=pltpu.PrefetchScalarGridSpec(
            num_scalar_prefetch=2,
            grid=(B_WIDTH // HEAD_GROUP, nch),
            in_specs=[smem, smem,
                      pl.BlockSpec((CHUNK, HEAD_GROUP), lambda hh, c, *_: (c, q0 + hh)),
                      prev(k0), main(k0), nxt(k0), prev(v0), main(v0), nxt(v0)],
            out_specs=pl.BlockSpec((CHUNK, HEAD_GROUP), lambda hh, c, *_: (c, hh)),
            scratch_shapes=[pltpu.VMEM((CHUNK + 2 * SWA_HALF, LANES), _BF16),
                            pltpu.VMEM((CHUNK + 2 * SWA_HALF, LANES), _BF16),
                            pltpu.VMEM((PAIRS_PER_GROUP, 4, 2 * ATT_TQ, win), _F32)],
        ),
        compiler_params=_cparams(2),
        name="swa_attn",
    )(*flags, slopes, sink, *([proj] * 7))


def _outproj_kernel(*refs, moe, parts_tiles):
    n_x = len(parts_tiles)
    x_refs, refs = refs[:n_x], refs[n_x:]
    (oa_ref, ob_ref, ga_ref, gb_ref, w_ref, g1_ref, sh2_ref, sc2_ref,
     lng_ref, lnb_ref) = refs[:10]
    if moe:
        wr_ref, x1_ref, h2_ref, rt_ref = refs[10:]
    else:
        x1_ref, h2_ref = refs[10:]
    x = _read_parts(x_refs, parts_tiles, pl.program_id(0))

    def rms(o_ref, g_ref):
        o = o_ref[...].astype(_F32)
        return o * lax.rsqrt(jnp.mean(o * o, axis=-1, keepdims=True) + LN_EPS) * g_ref[...]

    u = jnp.concatenate([rms(oa_ref, ga_ref), rms(ob_ref, gb_ref)], axis=-1).astype(_BF16)
    y = jnp.dot(u, w_ref[...], preferred_element_type=_F32)
    x1 = _layer_norm(DEEPNORM_ALPHA * x + g1_ref[0] * y, lng_ref[...], lnb_ref[...])
    x1_ref[...] = x1
    h2 = x1 * (1.0 + sc2_ref[0]) + sh2_ref[0]
    h2_ref[...] = h2.astype(h2_ref.dtype)
    if moe:
        logits = jnp.dot(h2.astype(_BF16), wr_ref[...], preferred_element_type=_F32)
        lane = lax.broadcasted_iota(jnp.int32, logits.shape, 1)
        valid = lane < N_EXPERTS
        logits = jnp.where(valid, logits, NEG_INF)
        e = jnp.exp(logits - jnp.max(logits, axis=-1, keepdims=True))
        probs = jnp.where(valid, e / jnp.sum(e, axis=-1, keepdims=True), -1.0)
        p1 = jnp.max(probs, axis=-1, keepdims=True)
        i1 = jnp.min(jnp.where(probs == p1, lane, LANES), axis=-1, keepdims=True)
        rest = jnp.where(lane == i1, -1.0, probs)
        p2 = jnp.max(rest, axis=-1, keepdims=True)
        i2 = jnp.min(jnp.where(rest == p2, lane, LANES), axis=-1, keepdims=True)
        den = p1 + p2
        rt = jnp.where(lane == 0, i1.astype(_F32), 0.0)
        rt = jnp.where(lane == 1, i2.astype(_F32), rt)
        rt = jnp.where(lane == 2, p1 / den, rt)
        rt = jnp.where(lane == 3, p2 / den, rt)
        rt_ref[...] = rt


def _outproj(oa, ob, g_a, g_b, w_out, x_parts, mod_l, ln_g, ln_b, w_router=None, tm=512):
    t = sum(p.shape[0] for p in x_parts)
    parts_tiles = tuple(p.shape[0] // tm for p in x_parts)
    moe = w_router is not None
    in_specs = _part_specs(x_parts, tm) + [
        _row_spec(tm, A_WIDTH), _row_spec(tm, B_WIDTH),
        _const_spec((1, A_WIDTH)), _const_spec((1, B_WIDTH)),
        _const_spec((A_WIDTH + B_WIDTH, D_MODEL)),
        _mod_spec(tm, 2), _mod_spec(tm, 3), _mod_spec(tm, 4),
        _const_spec((1, D_MODEL)), _const_spec((1, D_MODEL))]
    args = [*x_parts, oa, ob, g_a, g_b, w_out, mod_l, mod_l, mod_l, ln_g, ln_b]
    out_shape = [jax.ShapeDtypeStruct((t, D_MODEL), _F32),
                 jax.ShapeDtypeStruct((t, D_MODEL), _F32 if moe else _BF16)]
    out_specs = [_row_spec(tm, D_MODEL), _row_spec(tm, D_MODEL)]
    if moe:
        in_specs.append(_const_spec((D_MODEL, LANES)))
        args.append(w_router)
        out_shape.append(jax.ShapeDtypeStruct((t, LANES), _F32))
        out_specs.append(_row_spec(tm, LANES))
    return pl.pallas_call(
        functools.partial(_outproj_kernel, moe=moe, parts_tiles=parts_tiles),
        out_shape=tuple(out_shape),
        grid=(t // tm,),
        in_specs=in_specs,
        out_specs=tuple(out_specs),
        compiler_params=_cparams(1),
        name="out_proj_ln",
    )(*args)


def _ffn_kernel(*refs, n_f, epilogue, emit_h):
    te_ref, tv_ref, h_ref, wg_ref, wu_ref, wd_ref = refs[:6]
    rest = refs[6:]
    if epilogue:
        x_ref, g2_ref, lng_ref, lnb_ref = rest[:4]
        rest = rest[4:]
        if emit_h:
            sh_ref, sc_ref = rest[:2]
            rest = rest[2:]
    y_ref = rest[0]
    if emit_h:
        hn_ref = rest[1]
    i = pl.program_id(0)
    j = pl.program_id(1)
    valid = tv_ref[i] != 0

    def part():
        h = h_ref[...]
        g = jnp.dot(h, wg_ref[0], preferred_element_type=_F32)
        u = jnp.dot(h, wu_ref[0], preferred_element_type=_F32)
        a = (g * jax.nn.sigmoid(g) * u).astype(_BF16)
        return jnp.dot(a, wd_ref[0], preferred_element_type=_F32)

    @pl.when(jnp.logical_not(valid) & (j == 0))
    def _():
        y_ref[...] = jnp.zeros_like(y_ref)

    @pl.when(valid & (j == 0))
    def _():
        y_ref[...] = part()

    @pl.when(valid & (j > 0) & (j < n_f - 1))
    def _():
        y_ref[...] += part()

    @pl.when(valid & (j == n_f - 1))
    def _():
        y = y_ref[...] + part()
        if epilogue:
            y = _layer_norm(DEEPNORM_ALPHA * x_ref[...] + g2_ref[0] * y, lng_ref[...], lnb_ref[...])
            if emit_h:
                hn_ref[...] = (y * (1.0 + sc_ref[0]) + sh_ref[0]).astype(hn_ref.dtype)
        y_ref[...] = y


def _ffn(h, w_gate, w_up, w_down, tile_expert, tile_valid, *, tm, tf=512, ln=None):
    rows = h.shape[0]
    f = w_gate.shape[2]
    n_f = f // tf
    assert n_f >= 2
    in_specs = [pl.BlockSpec((tm, D_MODEL), lambda i, j, te, tv: (i, 0)),
                pl.BlockSpec((1, D_MODEL, tf), lambda i, j, te, tv: (te[i], 0, j)),
                pl.BlockSpec((1, D_MODEL, tf), lambda i, j, te, tv: (te[i], 0, j)),
                pl.BlockSpec((1, tf, D_MODEL), lambda i, j, te, tv: (te[i], j, 0))]
    args = [h, w_gate, w_up, w_down]
    out_shape = [jax.ShapeDtypeStruct((rows, D_MODEL), _F32)]
    out_specs = [_row_spec(tm, D_MODEL)]
    emit_h = False
    if ln is not None:
        x, mod_l, ln_g, ln_b, mod_next = ln
        in_specs += [_row_spec(tm, D_MODEL), _mod_spec(tm, 5),
                     _const_spec((1, D_MODEL)), _const_spec((1, D_MODEL))]
        args += [x, mod_l, ln_g, ln_b]
        emit_h = mod_next is not None
        if emit_h:
            in_specs += [_mod_spec(tm, 0), _mod_spec(tm, 1)]
            args += [mod_next, mod_next]
            out_shape.append(jax.ShapeDtypeStruct((rows, D_MODEL), _BF16))
            out_specs.append(_row_spec(tm, D_MODEL))
    out = pl.pallas_call(
        functools.partial(_ffn_kernel, n_f=n_f, epilogue=ln is not None, emit_h=emit_h),
        out_shape=tuple(out_shape),
        grid_spec=pltpu.PrefetchScalarGridSpec(
            num_scalar_prefetch=2,
            grid=(rows // tm, n_f),
            in_specs=in_specs,
            out_specs=tuple(out_specs),
        ),
        compiler_params=_cparams(2),
        name="ffn_fused",
    )(tile_expert, tile_valid, *args)
    return out if emit_h else (out[0], None)


def _start_row_copies(idx_ref, base, src_hbm, buf, sem, tm):
    def body(kk, c):
        for u in range(DMA_UNROLL):
            k = kk * DMA_UNROLL + u
            pltpu.make_async_copy(src_hbm.at[pl.ds(idx_ref[base + k], 1), :],
                                  buf.at[pl.ds(k, 1), :], sem).start()
        return c
    lax.fori_loop(0, tm // DMA_UNROLL, body, 0)


def _wait_row_copies(src_hbm, buf, sem, tm):
    pltpu.make_async_copy(src_hbm.at[pl.ds(0, tm), :], buf, sem).wait()


def _moe_ffn_kernel(te_ref, tv_ref, src_ref, h_hbm, wg_ref, wu_ref, wd_ref, y_ref,
                    xbuf, xb, sems, *, n_f, tm, first, chunk):
    i = pl.program_id(0)
    j = pl.program_id(1)
    slot = i % 2
    valid = tv_ref[i] != 0
    incoming = (i == 0) | (tv_ref[jnp.maximum(i - 1, 0)] != 0)

    def prefetch(k0, count):
        for u in range(count):
            k = k0 + u
            pltpu.make_async_copy(h_hbm.at[pl.ds(src_ref[(i + 1) * tm + k], 1), :],
                                  xbuf.at[1 - slot, pl.ds(k, 1), :], sems.at[1 - slot]).start()

    @pl.when((i == 0) & (j == 0))
    def _():
        _start_row_copies(src_ref, 0, h_hbm, xbuf.at[0], sems.at[0], tm)

    @pl.when(incoming & (j == 0))
    def _():
        _wait_row_copies(h_hbm, xbuf.at[slot], sems.at[slot], tm)
        xb[...] = xbuf[slot].astype(xb.dtype)

    def part():
        h = xb[...]
        g = jnp.dot(h, wg_ref[0], preferred_element_type=_F32)
        u = jnp.dot(h, wu_ref[0], preferred_element_type=_F32)
        a = (g * jax.nn.sigmoid(g) * u).astype(_BF16)
        return jnp.dot(a, wd_ref[0], preferred_element_type=_F32)

    @pl.when(jnp.logical_not(valid) & (j == 0))
    def _():
        y_ref[...] = jnp.zeros_like(y_ref)

    @pl.when(valid & (j == 0))
    def _():
        prefetch(0, first)
        y_ref[...] = part()

    @pl.when(valid & (j > 0) & (j < n_f - 1))
    def _():
        prefetch(first + (j - 1) * chunk, chunk)
        y_ref[...] += part()

    @pl.when(valid & (j == n_f - 1))
    def _():
        prefetch(first + (n_f - 2) * chunk, chunk)
        y_ref[...] += part()


def _moe_ffn(h, src_token, w_gate, w_up, w_down, tile_expert, tile_valid, tm, tf=512):
    rows = src_token.shape[0]
    f = w_gate.shape[2]
    n_f = f // tf
    chunk = (tm // (n_f + 2)) // DMA_UNROLL * DMA_UNROLL
    first = tm - chunk * (n_f - 1)
    assert n_f >= 3 and chunk > 0 and first > 0
    return pl.pallas_call(
        functools.partial(_moe_ffn_kernel, n_f=n_f, tm=tm, first=first, chunk=chunk),
        out_shape=jax.ShapeDtypeStruct((rows, D_MODEL), _F32),
        grid_spec=pltpu.PrefetchScalarGridSpec(
            num_scalar_prefetch=3,
            grid=(rows // tm, n_f),
            in_specs=[pl.BlockSpec(memory_space=pl.ANY),
                      pl.BlockSpec((1, D_MODEL, tf), lambda i, j, te, tv, src: (te[i], 0, j)),
                      pl.BlockSpec((1, D_MODEL, tf), lambda i, j, te, tv, src: (te[i], 0, j)),
                      pl.BlockSpec((1, tf, D_MODEL), lambda i, j, te, tv, src: (te[i], j, 0))],
            out_specs=pl.BlockSpec((tm, D_MODEL), lambda i, j, te, tv, src: (i, 0)),
            scratch_shapes=[pltpu.VMEM((2, tm, D_MODEL), _F32), pltpu.VMEM((tm, D_MODEL), _BF16),
                            pltpu.SemaphoreType.DMA((2,))],
        ),
        compiler_params=_cparams(2),
        name="moe_ffn",
    )(tile_expert, tile_valid, src_token, h, w_gate, w_up, w_down)


def _combine_kernel(pa_ref, pb_ref, ys_hbm, rt_ref, x_ref, g2_ref, lng_ref, lnb_ref,
                    o_ref, buf_a, buf_b, sem_a, sem_b, *, tm, tile0):
    i = pl.program_id(0) + tile0
    _start_row_copies(pa_ref, i * tm, ys_hbm, buf_a, sem_a, tm)
    _start_row_copies(pb_ref, i * tm, ys_hbm, buf_b, sem_b, tm)
    _wait_row_copies(ys_hbm, buf_a, sem_a, tm)
    _wait_row_copies(ys_hbm, buf_b, sem_b, tm)
    rt = rt_ref[...]
    y = rt[:, 2:3] * buf_a[...] + rt[:, 3:4] * buf_b[...]
    o_ref[...] = _layer_norm(DEEPNORM_ALPHA * x_ref[...] + g2_ref[0] * y,
                             lng_ref[...], lnb_ref[...])


def _combine_ln(ys, pos_a, pos_b, rt, x, mod_l, ln_g, ln_b, row0, n_rows, tm=256):
    assert row0 % tm == 0 and n_rows % tm == 0 and CHUNK % tm == 0
    tile0 = row0 // tm
    per_chunk = CHUNK // tm

    def rows(width):
        return pl.BlockSpec((tm, width), lambda i, *_: (i + tile0, 0))

    return pl.pallas_call(
        functools.partial(_combine_kernel, tm=tm, tile0=tile0),
        out_shape=jax.ShapeDtypeStruct((n_rows, D_MODEL), _F32),
        grid_spec=pltpu.PrefetchScalarGridSpec(
            num_scalar_prefetch=2,
            grid=(n_rows // tm,),
            in_specs=[pl.BlockSpec(memory_space=pl.ANY), rows(LANES), rows(D_MODEL),
                      pl.BlockSpec((1, 1, D_MODEL),
                                   lambda i, *_: ((i + tile0) // per_chunk, 0, 5)),
                      _const_spec((1, D_MODEL)), _const_spec((1, D_MODEL))],
            out_specs=_row_spec(tm, D_MODEL),
            scratch_shapes=[pltpu.VMEM((tm, D_MODEL), _F32), pltpu.VMEM((tm, D_MODEL), _F32),
                            pltpu.SemaphoreType.DMA, pltpu.SemaphoreType.DMA],
        ),
        compiler_params=_cparams(1),
        name="moe_combine_ln",
    )(pos_a, pos_b, ys, rt, x, mod_l, ln_g, ln_b)


def _route(rt, tm):
    t = rt.shape[0]
    n_tiles = (2 * t) // tm + N_EXPERTS + 1
    experts = rt[:, 0:2].astype(jnp.int32).reshape(-1)
    onehot = (experts[:, None] == jnp.arange(N_EXPERTS, dtype=jnp.int32)[None, :])
    csum = jnp.cumsum(onehot.astype(jnp.int32), axis=0)
    rank = jnp.take_along_axis(csum, experts[:, None], axis=1)[:, 0] - 1
    counts = csum[-1]
    tiles_per = (counts + tm - 1) // tm
    tile_end = jnp.cumsum(tiles_per)
    tile_start = tile_end - tiles_per
    pos = tile_start[experts] * tm + rank
    src_token = jnp.zeros((n_tiles * tm,), jnp.int32).at[pos].set(
        jnp.arange(2 * t, dtype=jnp.int32) // 2)
    tile_ids = jnp.arange(n_tiles, dtype=jnp.int32)
    tile_expert = jnp.minimum(
        jnp.sum((tile_ids[:, None] >= tile_end[None, :]).astype(jnp.int32), axis=1),
        N_EXPERTS - 1).astype(jnp.int32)
    tile_valid = (tile_ids < tile_end[-1]).astype(jnp.int32)
    last_expert = tile_expert[jnp.maximum(tile_end[-1] - 1, 0)]
    tile_expert = jnp.where(tile_valid != 0, tile_expert, last_expert)
    pos2 = pos.reshape(t, 2)
    return src_token, pos2[:, 0], pos2[:, 1], tile_expert, tile_valid


def _head_orders():
    perm = np.array([8 * hh + p + 4 * s for hh in range(2) for p in range(4) for s in range(2)])
    feat = (perm[:, None] * HEAD_DIM + np.arange(HEAD_DIM)[None, :]).reshape(-1)
    return perm, feat


def kernel(x_prompt, x_sample, c_prompt, c_sample, w_ada, b_ada, w_in, w_out, g_out, attn_sink,
           ln1_g, ln1_b, ln2_g, ln2_b, ffn_w_gate, ffn_w_up, ffn_w_down,
           moe_router, moe_w_gate, moe_w_up, moe_w_down):
    nb_p, s_p, _ = x_prompt.shape
    nb_s, s_s, _ = x_sample.shape
    t_p, t_s = nb_p * s_p, nb_s * s_s
    t = t_p + t_s
    groups = ((0, nb_p, s_p), (t_p, nb_s, s_s))
    for w, r in DILATED_PAIRS:
        assert w // (2 * r) == DIL_HALF and CHUNK % (r * ATT_TQ) == 0
    assert s_p % CHUNK == 0 and s_s % CHUNK == 0 and CHUNK % DIL_HALO == 0

    x = (x_prompt.reshape(t_p, D_MODEL), x_sample.reshape(t_s, D_MODEL))
    c_chunks = jnp.concatenate([jnp.repeat(c_prompt, s_p // CHUNK, axis=0),
                                jnp.repeat(c_sample, s_s // CHUNK, axis=0)], axis=0)
    nch = c_chunks.shape[0]
    mod = _ada(c_chunks, w_ada, b_ada).reshape(DEPTH, nch, 1, 6 * D_MODEL)
    flags = _chunk_flags(groups)

    perm_b, feat_b = _head_orders()
    slopes = 2.0 ** (-8.0 * np.arange(1, N_HEADS + 1) / N_HEADS)
    slopes_a = jnp.asarray(slopes.reshape(N_HEADS // 2, 2), _F32)
    slopes_b = jnp.asarray(slopes[perm_b].reshape(2, 8), _F32)
    qb0 = 3 * A_WIDTH
    in_cols = np.concatenate([np.arange(qb0), qb0 + feat_b, np.arange(qb0 + B_WIDTH, IN_WIDTH)])
    out_rows = np.concatenate([np.arange(A_WIDTH), A_WIDTH + feat_b])

    one_tile = jnp.zeros((t // FFN_TM,), jnp.int32), jnp.ones((t // FFN_TM,), jnp.int32)

    h = _modulate(x, mod[0])
    for l in range(DEPTH):
        mod_l = mod[l]
        w_in_l = w_in[l][:, in_cols].astype(_BF16)
        w_out_l = w_out[l][out_rows, :].astype(_BF16)
        g_l = g_out[l][out_rows]
        g_a, g_b = g_l[:A_WIDTH].reshape(1, A_WIDTH), g_l[A_WIDTH:].reshape(1, B_WIDTH)
        sink_b = attn_sink[l][perm_b].reshape(2, 8).astype(_F32)

        proj = _matmul(h, w_in_l)
        oa = _dilated_attention(proj, slopes_a, flags)
        ob = _swa_attention(proj, slopes_b, sink_b, flags)

        ln1 = ln1_g[l].reshape(1, D_MODEL), ln1_b[l].reshape(1, D_MODEL)
        ln2 = ln2_g[l].reshape(1, D_MODEL), ln2_b[l].reshape(1, D_MODEL)
        e = l // 2
        if l % 2 == 0:
            x1, h2 = _outproj(oa, ob, g_a, g_b, w_out_l, x, mod_l, *ln1)
            x2, h = _ffn(h2, ffn_w_gate[e:e + 1].astype(_BF16), ffn_w_up[e:e + 1].astype(_BF16),
                         ffn_w_down[e:e + 1].astype(_BF16), *one_tile, tm=FFN_TM,
                         ln=(x1, mod_l, *ln2, mod[l + 1] if l + 1 < DEPTH else None))
            x = (x2,)
        else:
            w_r = jnp.pad(moe_router[e], ((0, 0), (0, LANES - N_EXPERTS))).astype(_BF16)
            x, h2, rt = _outproj(oa, ob, g_a, g_b, w_out_l, x, mod_l, *ln1, w_router=w_r)
            src_token, pos_a, pos_b, tile_expert, tile_valid = _route(rt, MOE_TM)
            ys = _moe_ffn(h2, src_token, moe_w_gate[e].astype(_BF16), moe_w_up[e].astype(_BF16),
                          moe_w_down[e].astype(_BF16), tile_expert, tile_valid, MOE_TM)
            x = tuple(_combine_ln(ys, pos_a, pos_b, rt, x, mod_l, *ln2, row0=t0, n_rows=n * s)
                      for t0, n, s in groups)
            if l + 1 < DEPTH:
                h = _modulate(x, mod[l + 1])

    if len(x) == 1:
        x = (x[0][:t_p], x[0][t_p:])
    return x[0].reshape(nb_p, s_p, D_MODEL), x[1].reshape(nb_s, s_s, D_MODEL)
```

```python
import functools

import jax
import jax.numpy as jnp
import numpy as np
from jax import lax
from jax.experimental import pallas as pl
from jax.experimental.pallas import tpu as pltpu

D_MODEL = 2048
DEPTH = 2
HEAD_DIM = 64
N_HEADS = 16
B_KV_HEADS = 4
A_WIDTH = N_HEADS * HEAD_DIM
B_WIDTH = N_HEADS * HEAD_DIM
B_KV_WIDTH = B_KV_HEADS * HEAD_DIM
IN_WIDTH = 3 * A_WIDTH + B_WIDTH + 2 * B_KV_WIDTH
DILATED_PAIRS = ((128, 1), (512, 4), (2048, 16))
DILATIONS = tuple(sorted((r for _, r in DILATED_PAIRS), reverse=True))
assert DILATIONS[-1] == 1
DIL_HALF = DILATED_PAIRS[0][0] // 2
DIL_HALO = DIL_HALF * max(DILATIONS)
SWA_HALF = 128
N_EXPERTS = 8
DEEPNORM_ALPHA = (2.0 * DEPTH) ** 0.25
LN_EPS = 1e-5
NEG_INF = -1e30

CHUNK = 2048
LANES = 128
HEAD_GROUP = 512
PAIRS_PER_GROUP = HEAD_GROUP // LANES
ATT_TQ = 128
DIL_UNROLL = 16
SWA_UNROLL = 2
MOE_TM = 512
FFN_TM = 512
DMA_UNROLL = 8
VMEM_LIMIT = 56 * 1024 * 1024

_BF16 = jnp.bfloat16
_F32 = jnp.float32


def _cparams(n_axes):
    return pltpu.CompilerParams(dimension_semantics=("arbitrary",) * n_axes,
                                vmem_limit_bytes=VMEM_LIMIT)


def _layer_norm(z, g, b):
    mu = jnp.mean(z, axis=-1, keepdims=True)
    zc = z - mu
    var = jnp.mean(zc * zc, axis=-1, keepdims=True)
    return zc * lax.rsqrt(var + LN_EPS) * g + b


def _ada_kernel(c_ref, w_ref, b_ref, o_ref):
    c = c_ref[...]
    sc = (c * jax.nn.sigmoid(c)).astype(_BF16)
    w = w_ref[0].astype(_BF16)
    o_ref[0] = jnp.dot(sc, w, preferred_element_type=_F32) + b_ref[0]


def _ada(c_chunks, w_ada, b_ada):
    nch = c_chunks.shape[0]
    tn = 1024
    n_out = w_ada.shape[2]
    return pl.pallas_call(
        _ada_kernel,
        out_shape=jax.ShapeDtypeStruct((DEPTH, nch, n_out), _F32),
        grid=(DEPTH, n_out // tn),
        in_specs=[
            pl.BlockSpec((nch, D_MODEL), lambda l, j: (0, 0)),
            pl.BlockSpec((1, D_MODEL, tn), lambda l, j: (l, 0, j)),
            pl.BlockSpec((1, 1, tn), lambda l, j: (l, 0, j)),
        ],
        out_specs=pl.BlockSpec((1, nch, tn), lambda l, j: (l, 0, j)),
        compiler_params=_cparams(2),
        name="ada_mod",
    )(c_chunks, w_ada, b_ada.reshape(DEPTH, 1, n_out))


def _part_specs(parts, tm):
    specs, start = [], 0
    for p in parts:
        n = p.shape[0] // tm
        specs.append(pl.BlockSpec(
            (tm, p.shape[1]),
            lambda i, *_, start=start, n=n: (jnp.clip(i - start, 0, n - 1), 0)))
        start += n
    return specs


def _read_parts(refs, parts_tiles, i):
    x = refs[0][...]
    bound = 0
    for k in range(1, len(refs)):
        bound += parts_tiles[k - 1]
        x = jnp.where(i >= bound, refs[k][...], x)
    return x


def _modulate_kernel(*refs, parts_tiles):
    n = len(parts_tiles)
    sh_ref, sc_ref, o_ref = refs[n:]
    x = _read_parts(refs[:n], parts_tiles, pl.program_id(0))
    o_ref[...] = (x * (1.0 + sc_ref[0]) + sh_ref[0]).astype(o_ref.dtype)


def _mod_spec(tm, which):
    assert CHUNK % tm == 0
    per_chunk = CHUNK // tm
    return pl.BlockSpec((1, 1, D_MODEL), lambda i, *_: (i // per_chunk, 0, which))


def _row_spec(tm, width):
    return pl.BlockSpec((tm, width), lambda i, *_: (i, 0))


def _const_spec(shape):
    return pl.BlockSpec(shape, lambda *_: (0,) * len(shape))


def _modulate(x_parts, mod_l, tm=1024):
    t = sum(p.shape[0] for p in x_parts)
    parts_tiles = tuple(p.shape[0] // tm for p in x_parts)
    return pl.pallas_call(
        functools.partial(_modulate_kernel, parts_tiles=parts_tiles),
        out_shape=jax.ShapeDtypeStruct((t, D_MODEL), _BF16),
        grid=(t // tm,),
        in_specs=_part_specs(x_parts, tm) + [_mod_spec(tm, 0), _mod_spec(tm, 1)],
        out_specs=_row_spec(tm, D_MODEL),
        compiler_params=_cparams(1),
        name="modulate",
    )(*x_parts, mod_l, mod_l)


def _matmul_kernel(x_ref, w_ref, o_ref):
    o_ref[...] = jnp.dot(x_ref[...], w_ref[...],
                         preferred_element_type=_F32).astype(o_ref.dtype)


def _matmul(x, w, tm=1024, tn=1536):
    t, k = x.shape
    n = w.shape[1]
    return pl.pallas_call(
        _matmul_kernel,
        out_shape=jax.ShapeDtypeStruct((t, n), _BF16),
        grid=(n // tn, t // tm),
        in_specs=[pl.BlockSpec((tm, k), lambda j, i: (i, 0)),
                  pl.BlockSpec((k, tn), lambda j, i: (0, j))],
        out_specs=pl.BlockSpec((tm, tn), lambda j, i: (i, j)),
        compiler_params=_cparams(2),
        name="in_proj",
    )(x, w)


def _bias_tables(bias_ref, lead, slopes, half, dist_scale):
    tq = ATT_TQ
    win = tq + 2 * half
    row = lax.broadcasted_iota(jnp.int32, (tq, win), 0)
    col = lax.broadcasted_iota(jnp.int32, (tq, win), 1)
    dist = jnp.abs(col - half - row)
    scaled = dist.astype(_F32) * dist_scale
    band = dist <= half
    for variant in range(4):
        ok = band
        if variant & 1:
            ok = ok & (col >= half)
        if variant & 2:
            ok = ok & (col < tq + half)
        for s in range(2):
            bias_ref[lead + (variant, slice(s * tq, (s + 1) * tq), slice(None))] = jnp.where(
                ok, -(slopes[s] * scaled), NEG_INF)


def _pair_scores(q2, kwin, bias, lo):
    zero = jnp.zeros_like(q2)
    qq = jnp.concatenate([jnp.where(lo, q2, zero), jnp.where(lo, zero, q2)], axis=0)
    s = lax.dot_general(qq.astype(_BF16), kwin, (((1,), (1,)), ((), ())),
                        preferred_element_type=_F32)
    return s + bias


def _by_head(col, lo):
    tq = col.shape[0] // 2
    return jnp.where(lo, col[:tq], col[tq:])


def _chunk_flags(groups):
    first, last = [], []
    for _, n_seq, seq in groups:
        per = seq // CHUNK
        for _ in range(n_seq):
            first += [1] + [0] * (per - 1)
            last += [0] * (per - 1) + [1]
    return jnp.asarray(first, jnp.int32), jnp.asarray(last, jnp.int32)


def _dilated_kernel(first_ref, last_ref, slope_ref, q_ref, kp_ref, km_ref, kn_ref,
                    vp_ref, vm_ref, vn_ref, o_ref, qf, kf, vf, acc, mst, lst, bias_ref):
    tq = ATT_TQ
    half = DIL_HALF
    win = tq + 2 * half
    pair = pl.program_id(0)
    c = pl.program_id(1)

    @pl.when(c == 0)
    def _():
        slopes = (slope_ref[pair, 0], slope_ref[pair, 1])
        for b, r in enumerate(DILATIONS):
            _bias_tables(bias_ref, (b,), slopes, half, float(r))

    qf[...] = q_ref[...].astype(_F32) * (HEAD_DIM ** -0.5)
    for dst, (p_ref, m_ref, n_ref) in ((kf, (kp_ref, km_ref, kn_ref)), (vf, (vp_ref, vm_ref, vn_ref))):
        dst[0:DIL_HALO, :] = p_ref[...].astype(_F32)
        dst[DIL_HALO:DIL_HALO + CHUNK, :] = m_ref[...].astype(_F32)
        dst[DIL_HALO + CHUNK:, :] = n_ref[...].astype(_F32)

    is_first = first_ref[c] != 0
    is_last = last_ref[c] != 0
    lane = lax.broadcasted_iota(jnp.int32, (tq, LANES), 1)
    lo = lane < HEAD_DIM

    def rows(start, size, r):
        if r == 1:
            return pl.ds(pl.multiple_of(start, half), size)
        return pl.ds(start, size, stride=r)

    def local_softmax(b, r, rho, sb, n_sub):
        i0 = sb * tq
        qrows = rows(rho + r * i0, tq, r)
        krows = rows(DIL_HALO + rho + r * (i0 - half), win, r)
        variant = ((is_first & (sb == 0)).astype(jnp.int32)
                   + 2 * (is_last & (sb == n_sub - 1)).astype(jnp.int32))
        s = _pair_scores(qf[qrows, :], kf[krows, :].astype(_BF16), bias_ref[b, variant], lo)
        m_cur = jnp.max(s, axis=-1, keepdims=True)
        pr = jnp.exp(s - m_cur)
        l_cur = jnp.sum(pr, axis=-1, keepdims=True)
        pv = _by_head(jnp.dot(pr.astype(_BF16), vf[krows, :].astype(_BF16),
                              preferred_element_type=_F32), lo)
        return qrows, pv, _by_head(m_cur, lo), _by_head(l_cur, lo)

    def merge(b, qrows, pv, m_h, l_h):
        if b == 0:
            acc[qrows, :] = pv
            mst[qrows, :] = m_h
            lst[qrows, :] = l_h
            return
        m_p = mst[qrows, :]
        m_n = jnp.maximum(m_p, m_h)
        a_p = jnp.exp(m_p - m_n)
        a_c = jnp.exp(m_h - m_n)
        acc_n = acc[qrows, :] * a_p + pv * a_c
        l_n = lst[qrows, :] * a_p + l_h * a_c
        if b == len(DILATIONS) - 1:
            o_ref[qrows, :] = (acc_n / l_n).astype(o_ref.dtype)
        else:
            acc[qrows, :] = acc_n
            mst[qrows, :] = m_n
            lst[qrows, :] = l_n

    for b, r in enumerate(DILATIONS):
        n_sub = CHUNK // (r * tq)

        def body(it, carry, b=b, r=r, n_sub=n_sub):
            parts = []
            for u in range(DIL_UNROLL):
                idx = it * DIL_UNROLL + u
                parts.append(local_softmax(b, r, idx // n_sub, idx % n_sub, n_sub))
            for part in parts:
                merge(b, *part)
            return carry
        lax.fori_loop(0, r * n_sub // DIL_UNROLL, body, 0)


def _dilated_attention(proj, slopes, flags):
    t = proj.shape[0]
    nch = t // CHUNK
    per = CHUNK // DIL_HALO
    n_pairs = A_WIDTH // LANES
    win = ATT_TQ + 2 * DIL_HALF

    def main(col0):
        return pl.BlockSpec((CHUNK, LANES), lambda p, c, *_: (c, col0 + p))

    def prev(col0):
        return pl.BlockSpec((DIL_HALO, LANES),
                            lambda p, c, *_: (jnp.maximum(c * per - 1, 0), col0 + p))

    def nxt(col0):
        return pl.BlockSpec((DIL_HALO, LANES),
                            lambda p, c, *_: (jnp.minimum((c + 1) * per, nch * per - 1), col0 + p))

    k0, v0 = A_WIDTH // LANES, 2 * A_WIDTH // LANES
    return pl.pallas_call(
        _dilated_kernel,
        out_shape=jax.ShapeDtypeStruct((t, A_WIDTH), _BF16),
        grid_spec=pltpu.PrefetchScalarGridSpec(
            num_scalar_prefetch=2,
            grid=(n_pairs, nch),
            in_specs=[pl.BlockSpec(memory_space=pltpu.SMEM), main(0),
                      prev(k0), main(k0), nxt(k0), prev(v0), main(v0), nxt(v0)],
            out_specs=main(0),
            scratch_shapes=[pltpu.VMEM((CHUNK, LANES), _F32),
                            pltpu.VMEM((CHUNK + 2 * DIL_HALO, LANES), _F32),
                            pltpu.VMEM((CHUNK + 2 * DIL_HALO, LANES), _F32),
                            pltpu.VMEM((CHUNK, LANES), _F32),
                            pltpu.VMEM((CHUNK, LANES), _F32),
                            pltpu.VMEM((CHUNK, LANES), _F32),
                            pltpu.VMEM((len(DILATIONS), 4, 2 * ATT_TQ, win), _F32)],
        ),
        compiler_params=_cparams(2),
        name="dilated_attn",
    )(*flags, slopes, *([proj] * 7))


def _swa_kernel(first_ref, last_ref, slope_ref, sink_ref, q_ref, kp_ref, km_ref, kn_ref,
                vp_ref, vm_ref, vn_ref, o_ref, kbuf, vbuf, bias_ref):
    tq = ATT_TQ
    half = SWA_HALF
    win = tq + 2 * half
    n_sub = CHUNK // tq
    hh = pl.program_id(0)
    c = pl.program_id(1)

    @pl.when(c == 0)
    def _():
        for p in range(PAIRS_PER_GROUP):
            _bias_tables(bias_ref, (p,), (slope_ref[hh, 2 * p], slope_ref[hh, 2 * p + 1]),
                         half, 1.0)

    for dst, (p_ref, m_ref, n_ref) in ((kbuf, (kp_ref, km_ref, kn_ref)),
                                       (vbuf, (vp_ref, vm_ref, vn_ref))):
        dst[0:half, :] = p_ref[...]
        dst[half:half + CHUNK, :] = m_ref[...]
        dst[half + CHUNK:, :] = n_ref[...]

    is_first = first_ref[c] != 0
    is_last = last_ref[c] != 0
    lane = lax.broadcasted_iota(jnp.int32, (tq, LANES), 1)
    lo = lane < HEAD_DIM
    row2 = lax.broadcasted_iota(jnp.int32, (2 * tq, 1), 0)

    def sub_block(sb):
        r0 = pl.multiple_of(sb * tq, tq)
        variant = ((is_first & (sb == 0)).astype(jnp.int32)
                   + 2 * (is_last & (sb == n_sub - 1)).astype(jnp.int32))
        kwin = kbuf[pl.ds(r0, win), :]
        vwin = vbuf[pl.ds(r0, win), :]
        for p in range(PAIRS_PER_GROUP):
            ql = slice(p * LANES, (p + 1) * LANES)
            q2 = q_ref[pl.ds(r0, tq), ql] * jnp.asarray(HEAD_DIM ** -0.5, _BF16)
            s = _pair_scores(q2, kwin, bias_ref[p, variant], lo)
            sink = jnp.where(row2 < tq, sink_ref[hh, 2 * p], sink_ref[hh, 2 * p + 1])
            m_new = jnp.maximum(jnp.max(s, axis=-1, keepdims=True), sink)
            pr = jnp.exp(s - m_new)
            l_new = jnp.exp(sink - m_new) + jnp.sum(pr, axis=-1, keepdims=True)
            pv = _by_head(jnp.dot(pr.astype(_BF16), vwin, preferred_element_type=_F32), lo)
            o_ref[pl.ds(r0, tq), ql] = (pv * _by_head(1.0 / l_new, lo)).astype(o_ref.dtype)

    def body(it, carry):
        for u in range(SWA_UNROLL):
            sub_block(it * SWA_UNROLL + u)
        return carry

    lax.fori_loop(0, n_sub // SWA_UNROLL, body, 0)


def _swa_attention(proj, slopes, sink, flags):
    t = proj.shape[0]
    nch = t // CHUNK
    per = CHUNK // SWA_HALF
    win = ATT_TQ + 2 * SWA_HALF
    q0 = 3 * A_WIDTH // HEAD_GROUP
    k0 = (3 * A_WIDTH + B_WIDTH) // LANES
    v0 = k0 + B_KV_WIDTH // LANES

    def main(col0):
        return pl.BlockSpec((CHUNK, LANES), lambda hh, c, *_: (c, col0 + hh))

    def prev(col0):
        return pl.BlockSpec((SWA_HALF, LANES),
                            lambda hh, c, *_: (jnp.maximum(c * per - 1, 0), col0 + hh))

    def nxt(col0):
        return pl.BlockSpec((SWA_HALF, LANES),
                            lambda hh, c, *_: (jnp.minimum((c + 1) * per, nch * per - 1), col0 + hh))

    smem = pl.BlockSpec(memory_space=pltpu.SMEM)
    return pl.pallas_call(
        _swa_kernel,
        out_shape=jax.ShapeDtypeStruct((t, B_WIDTH), _BF16),
        grid_spec=pltpu.PrefetchScalarGridSpec(
            num_scalar_prefetch=2,
            grid=(B_WIDTH // HEAD_GROUP, nch),
            in_specs=[smem, smem,
                      pl.BlockSpec((CHUNK, HEAD_GROUP), lambda hh, c, *_: (c, q0 + hh)),
                      prev(k0), main(k0), nxt(k0), prev(v0), main(v0), nxt(v0)],
            out_specs=pl.BlockSpec((CHUNK, HEAD_GROUP), lambda hh, c, *_: (c, hh)),
            scratch_shapes=[pltpu.VMEM((CHUNK + 2 * SWA_HALF, LANES), _BF16),
                            pltpu.VMEM((CHUNK + 2 * SWA_HALF, LANES), _BF16),
                            pltpu.VMEM((PAIRS_PER_GROUP, 4, 2 * ATT_TQ, win), _F32)],
        ),
        compiler_params=_cparams(2),
        name="swa_attn",
    )(*flags, slopes, sink, *([proj] * 7))


def _outproj_kernel(*refs, moe, parts_tiles):
    n_x = len(parts_tiles)
    x_refs, refs = refs[:n_x], refs[n_x:]
    (oa_ref, ob_ref, ga_ref, gb_ref, w_ref, g1_ref, sh2_ref, sc2_ref,
     lng_ref, lnb_ref) = refs[:10]
    if moe:
        wr_ref, x1_ref, h2_ref, rt_ref = refs[10:]
    else:
        x1_ref, h2_ref = refs[10:]
    x = _read_parts(x_refs, parts_tiles, pl.program_id(0))

    def rms(o_ref, g_ref):
        o = o_ref[...].astype(_F32)
        return o * lax.rsqrt(jnp.mean(o * o, axis=-1, keepdims=True) + LN_EPS) * g_ref[...]

    u = jnp.concatenate([rms(oa_ref, ga_ref), rms(ob_ref, gb_ref)], axis=-1).astype(_BF16)
    y = jnp.dot(u, w_ref[...], preferred_element_type=_F32)
    x1 = _layer_norm(DEEPNORM_ALPHA * x + g1_ref[0] * y, lng_ref[...], lnb_ref[...])
    x1_ref[...] = x1
    h2 = x1 * (1.0 + sc2_ref[0]) + sh2_ref[0]
    h2_ref[...] = h2.astype(h2_ref.dtype)
    if moe:
        logits = jnp.dot(h2.astype(_BF16), wr_ref[...], preferred_element_type=_F32)
        lane = lax.broadcasted_iota(jnp.int32, logits.shape, 1)
        valid = lane < N_EXPERTS
        logits = jnp.where(valid, logits, NEG_INF)
        e = jnp.exp(logits - jnp.max(logits, axis=-1, keepdims=True))
        probs = jnp.where(valid, e / jnp.sum(e, axis=-1, keepdims=True), -1.0)
        p1 = jnp.max(probs, axis=-1, keepdims=True)
        i1 = jnp.min(jnp.where(probs == p1, lane, LANES), axis=-1, keepdims=True)
        rest = jnp.where(lane == i1, -1.0, probs)
        p2 = jnp.max(rest, axis=-1, keepdims=True)
        i2 = jnp.min(jnp.where(rest == p2, lane, LANES), axis=-1, keepdims=True)
        den = p1 + p2
        rt = jnp.where(lane == 0, i1.astype(_F32), 0.0)
        rt = jnp.where(lane == 1, i2.astype(_F32), rt)
        rt = jnp.where(lane == 2, p1 / den, rt)
        rt = jnp.where(lane == 3, p2 / den, rt)
        rt_ref[...] = rt


def _outproj(oa, ob, g_a, g_b, w_out, x_parts, mod_l, ln_g, ln_b, w_router=None, tm=512):
    t = sum(p.shape[0] for p in x_parts)
    parts_tiles = tuple(p.shape[0] // tm for p in x_parts)
    moe = w_router is not None
    in_specs = _part_specs(x_parts, tm) + [
        _row_spec(tm, A_WIDTH), _row_spec(tm, B_WIDTH),
        _const_spec((1, A_WIDTH)), _const_spec((1, B_WIDTH)),
        _const_spec((A_WIDTH + B_WIDTH, D_MODEL)),
        _mod_spec(tm, 2), _mod_spec(tm, 3), _mod_spec(tm, 4),
        _const_spec((1, D_MODEL)), _const_spec((1, D_MODEL))]
    args = [*x_parts, oa, ob, g_a, g_b, w_out, mod_l, mod_l, mod_l, ln_g, ln_b]
    out_shape = [jax.ShapeDtypeStruct((t, D_MODEL), _F32),
                 jax.ShapeDtypeStruct((t, D_MODEL), _F32 if moe else _BF16)]
    out_specs = [_row_spec(tm, D_MODEL), _row_spec(tm, D_MODEL)]
    if moe:
        in_specs.append(_const_spec((D_MODEL, LANES)))
        args.append(w_router)
        out_shape.append(jax.ShapeDtypeStruct((t, LANES), _F32))
        out_specs.append(_row_spec(tm, LANES))
    return pl.pallas_call(
        functools.partial(_outproj_kernel, moe=moe, parts_tiles=parts_tiles),
        out_shape=tuple(out_shape),
        grid=(t // tm,),
        in_specs=in_specs,
        out_specs=tuple(out_specs),
        compiler_params=_cparams(1),
        name="out_proj_ln",
    )(*args)


def _ffn_kernel(*refs, n_f, epilogue, emit_h):
    te_ref, tv_ref, h_ref, wg_ref, wu_ref, wd_ref = refs[:6]
    rest = refs[6:]
    if epilogue:
        x_ref, g2_ref, lng_ref, lnb_ref = rest[:4]
        rest = rest[4:]
        if emit_h:
            sh_ref, sc_ref = rest[:2]
            rest = rest[2:]
    y_ref = rest[0]
    if emit_h:
        hn_ref = rest[1]
    i = pl.program_id(0)
    j = pl.program_id(1)
    valid = tv_ref[i] != 0

    def part():
        h = h_ref[...]
        g = jnp.dot(h, wg_ref[0], preferred_element_type=_F32)
        u = jnp.dot(h, wu_ref[0], preferred_element_type=_F32)
        a = (g * jax.nn.sigmoid(g) * u).astype(_BF16)
        return jnp.dot(a, wd_ref[0], preferred_element_type=_F32)

    @pl.when(jnp.logical_not(valid) & (j == 0))
    def _():
        y_ref[...] = jnp.zeros_like(y_ref)

    @pl.when(valid & (j == 0))
    def _():
        y_ref[...] = part()

    @pl.when(valid & (j > 0) & (j < n_f - 1))
    def _():
        y_ref[...] += part()

    @pl.when(valid & (j == n_f - 1))
    def _():
        y = y_ref[...] + part()
        if epilogue:
            y = _layer_norm(DEEPNORM_ALPHA * x_ref[...] + g2_ref[0] * y, lng_ref[...], lnb_ref[...])
            if emit_h:
                hn_ref[...] = (y * (1.0 + sc_ref[0]) + sh_ref[0]).astype(hn_ref.dtype)
        y_ref[...] = y


def _ffn(h, w_gate, w_up, w_down, tile_expert, tile_valid, *, tm, tf=512, ln=None):
    rows = h.shape[0]
    f = w_gate.shape[2]
    n_f = f // tf
    assert n_f >= 2
    in_specs = [pl.BlockSpec((tm, D_MODEL), lambda i, j, te, tv: (i, 0)),
                pl.BlockSpec((1, D_MODEL, tf), lambda i, j, te, tv: (te[i], 0, j)),
                pl.BlockSpec((1, D_MODEL, tf), lambda i, j, te, tv: (te[i], 0, j)),
                pl.BlockSpec((1, tf, D_MODEL), lambda i, j, te, tv: (te[i], j, 0))]
    args = [h, w_gate, w_up, w_down]
    out_shape = [jax.ShapeDtypeStruct((rows, D_MODEL), _F32)]
    out_specs = [_row_spec(tm, D_MODEL)]
    emit_h = False
    if ln is not None:
        x, mod_l, ln_g, ln_b, mod_next = ln
        in_specs += [_row_spec(tm, D_MODEL), _mod_spec(tm, 5),
                     _const_spec((1, D_MODEL)), _const_spec((1, D_MODEL))]
        args += [x, mod_l, ln_g, ln_b]
        emit_h = mod_next is not None
        if emit_h:
            in_specs += [_mod_spec(tm, 0), _mod_spec(tm, 1)]
            args += [mod_next, mod_next]
            out_shape.append(jax.ShapeDtypeStruct((rows, D_MODEL), _BF16))
            out_specs.append(_row_spec(tm, D_MODEL))
    out = pl.pallas_call(
        functools.partial(_ffn_kernel, n_f=n_f, epilogue=ln is not None, emit_h=emit_h),
        out_shape=tuple(out_shape),
        grid_spec=pltpu.PrefetchScalarGridSpec(
            num_scalar_prefetch=2,
            grid=(rows // tm, n_f),
            in_specs=in_specs,
            out_specs=tuple(out_specs),
        ),
        compiler_params=_cparams(2),
        name="ffn_fused",
    )(tile_expert, tile_valid, *args)
    return out if emit_h else (out[0], None)


def _start_row_copies(idx_ref, base, src_hbm, buf, sem, tm):
    def body(kk, c):
        for u in range(DMA_UNROLL):
            k = kk * DMA_UNROLL + u
            pltpu.make_async_copy(src_hbm.at[pl.ds(idx_ref[base + k], 1), :],
                                  buf.at[pl.ds(k, 1), :], sem).start()
        return c
    lax.fori_loop(0, tm // DMA_UNROLL, body, 0)


def _wait_row_copies(src_hbm, buf, sem, tm):
    pltpu.make_async_copy(src_hbm.at[pl.ds(0, tm), :], buf, sem).wait()


def _moe_ffn_kernel(te_ref, tv_ref, src_ref, h_hbm, wg_ref, wu_ref, wd_ref, y_ref,
                    xbuf, xb, sems, *, n_f, tm, first, chunk):
    i = pl.program_id(0)
    j = pl.program_id(1)
    slot = i % 2
    valid = tv_ref[i] != 0
    incoming = (i == 0) | (tv_ref[jnp.maximum(i - 1, 0)] != 0)

    def prefetch(k0, count):
        for u in range(count):
            k = k0 + u
            pltpu.make_async_copy(h_hbm.at[pl.ds(src_ref[(i + 1) * tm + k], 1), :],
                                  xbuf.at[1 - slot, pl.ds(k, 1), :], sems.at[1 - slot]).start()

    @pl.when((i == 0) & (j == 0))
    def _():
        _start_row_copies(src_ref, 0, h_hbm, xbuf.at[0], sems.at[0], tm)

    @pl.when(incoming & (j == 0))
    def _():
        _wait_row_copies(h_hbm, xbuf.at[slot], sems.at[slot], tm)
        xb[...] = xbuf[slot].astype(xb.dtype)

    def part():
        h = xb[...]
        g = jnp.dot(h, wg_ref[0], preferred_element_type=_F32)
        u = jnp.dot(h, wu_ref[0], preferred_element_type=_F32)
        a = (g * jax.nn.sigmoid(g) * u).astype(_BF16)
        return jnp.dot(a, wd_ref[0], preferred_element_type=_F32)

    @pl.when(jnp.logical_not(valid) & (j == 0))
    def _():
        y_ref[...] = jnp.zeros_like(y_ref)

    @pl.when(valid & (j == 0))
    def _():
        prefetch(0, first)
        y_ref[...] = part()

    @pl.when(valid & (j > 0) & (j < n_f - 1))
    def _():
        prefetch(first + (j - 1) * chunk, chunk)
        y_ref[...] += part()

    @pl.when(valid & (j == n_f - 1))
    def _():
        prefetch(first + (n_f - 2) * chunk, chunk)
        y_ref[...] += part()


def _moe_ffn(h, src_token, w_gate, w_up, w_down, tile_expert, tile_valid, tm, tf=512):
    rows = src_token.shape[0]
    f = w_gate.shape[2]
    n_f = f // tf
    chunk = (tm // (n_f + 2)) // DMA_UNROLL * DMA_UNROLL
    first = tm - chunk * (n_f - 1)
    assert n_f >= 3 and chunk > 0 and first > 0
    return pl.pallas_call(
        functools.partial(_moe_ffn_kernel, n_f=n_f, tm=tm, first=first, chunk=chunk),
        out_shape=jax.ShapeDtypeStruct((rows, D_MODEL), _F32),
        grid_spec=pltpu.PrefetchScalarGridSpec(
            num_scalar_prefetch=3,
            grid=(rows // tm, n_f),
            in_specs=[pl.BlockSpec(memory_space=pl.ANY),
                      pl.BlockSpec((1, D_MODEL, tf), lambda i, j, te, tv, src: (te[i], 0, j)),
                      pl.BlockSpec((1, D_MODEL, tf), lambda i, j, te, tv, src: (te[i], 0, j)),
                      pl.BlockSpec((1, tf, D_MODEL), lambda i, j, te, tv, src: (te[i], j, 0))],
            out_specs=pl.BlockSpec((tm, D_MODEL), lambda i, j, te, tv, src: (i, 0)),
            scratch_shapes=[pltpu.VMEM((2, tm, D_MODEL), _F32), pltpu.VMEM((tm, D_MODEL), _BF16),
                            pltpu.SemaphoreType.DMA((2,))],
        ),
        compiler_params=_cparams(2),
        name="moe_ffn",
    )(tile_expert, tile_valid, src_token, h, w_gate, w_up, w_down)


def _combine_kernel(pa_ref, pb_ref, ys_hbm, rt_ref, x_ref, g2_ref, lng_ref, lnb_ref,
                    o_ref, buf_a, buf_b, sem_a, sem_b, *, tm, tile0):
    i = pl.program_id(0) + tile0
    _start_row_copies(pa_ref, i * tm, ys_hbm, buf_a, sem_a, tm)
    _start_row_copies(pb_ref, i * tm, ys_hbm, buf_b, sem_b, tm)
    _wait_row_copies(ys_hbm, buf_a, sem_a, tm)
    _wait_row_copies(ys_hbm, buf_b, sem_b, tm)
    rt = rt_ref[...]
    y = rt[:, 2:3] * buf_a[...] + rt[:, 3:4] * buf_b[...]
    o_ref[...] = _layer_norm(DEEPNORM_ALPHA * x_ref[...] + g2_ref[0] * y,
                             lng_ref[...], lnb_ref[...])


def _combine_ln(ys, pos_a, pos_b, rt, x, mod_l, ln_g, ln_b, row0, n_rows, tm=256):
    assert row0 % tm == 0 and n_rows % tm == 0 and CHUNK % tm == 0
    tile0 = row0 // tm
    per_chunk = CHUNK // tm

    def rows(width):
        return pl.BlockSpec((tm, width), lambda i, *_: (i + tile0, 0))

    return pl.pallas_call(
        functools.partial(_combine_kernel, tm=tm, tile0=tile0),
        out_shape=jax.ShapeDtypeStruct((n_rows, D_MODEL), _F32),
        grid_spec=pltpu.PrefetchScalarGridSpec(
            num_scalar_prefetch=2,
            grid=(n_rows // tm,),
            in_specs=[pl.BlockSpec(memory_space=pl.ANY), rows(LANES), rows(D_MODEL),
                      pl.BlockSpec((1, 1, D_MODEL),
                                   lambda i, *_: ((i + tile0) // per_chunk, 0, 5)),
                      _const_spec((1, D_MODEL)), _const_spec((1, D_MODEL))],
            out_specs=_row_spec(tm, D_MODEL),
            scratch_shapes=[pltpu.VMEM((tm, D_MODEL), _F32), pltpu.VMEM((tm, D_MODEL), _F32),
                            pltpu.SemaphoreType.DMA, pltpu.SemaphoreType.DMA],
        ),
        compiler_params=_cparams(1),
        name="moe_combine_ln",
    )(pos_a, pos_b, ys, rt, x, mod_l, ln_g, ln_b)


def _route(rt, tm):
    t = rt.shape[0]
    n_tiles = (2 * t) // tm + N_EXPERTS + 1
    experts = rt[:, 0:2].astype(jnp.int32).reshape(-1)
    onehot = (experts[:, None] == jnp.arange(N_EXPERTS, dtype=jnp.int32)[None, :])
    csum = jnp.cumsum(onehot.astype(jnp.int32), axis=0)
    rank = jnp.take_along_axis(csum, experts[:, None], axis=1)[:, 0] - 1
    counts = csum[-1]
    tiles_per = (counts + tm - 1) // tm
    tile_end = jnp.cumsum(tiles_per)
    tile_start = tile_end - tiles_per
    pos = tile_start[experts] * tm + rank
    src_token = jnp.zeros((n_tiles * tm,), jnp.int32).at[pos].set(
        jnp.arange(2 * t, dtype=jnp.int32) // 2)
    tile_ids = jnp.arange(n_tiles, dtype=jnp.int32)
    tile_expert = jnp.minimum(
        jnp.sum((tile_ids[:, None] >= tile_end[None, :]).astype(jnp.int32), axis=1),
        N_EXPERTS - 1).astype(jnp.int32)
    tile_valid = (tile_ids < tile_end[-1]).astype(jnp.int32)
    last_expert = tile_expert[jnp.maximum(tile_end[-1] - 1, 0)]
    tile_expert = jnp.where(tile_valid != 0, tile_expert, last_expert)
    pos2 = pos.reshape(t, 2)
    return src_token, pos2[:, 0], pos2[:, 1], tile_expert, tile_valid


def _head_orders():
    perm = np.array([8 * hh + p + 4 * s for hh in range(2) for p in range(4) for s in range(2)])
    feat = (perm[:, None] * HEAD_DIM + np.arange(HEAD_DIM)[None, :]).reshape(-1)
    return perm, feat


def kernel(x_prompt, x_sample, c_prompt, c_sample, w_ada, b_ada, w_in, w_out, g_out, attn_sink,
           ln1_g, ln1_b, ln2_g, ln2_b, ffn_w_gate, ffn_w_up, ffn_w_down,
           moe_router, moe_w_gate, moe_w_up, moe_w_down):
    nb_p, s_p, _ = x_prompt.shape
    nb_s, s_s, _ = x_sample.shape
    t_p, t_s = nb_p * s_p, nb_s * s_s
    t = t_p + t_s
    groups = ((0, nb_p, s_p), (t_p, nb_s, s_s))
    for w, r in DILATED_PAIRS:
        assert w // (2 * r) == DIL_HALF and CHUNK % (r * ATT_TQ) == 0
    assert s_p % CHUNK == 0 and s_s % CHUNK == 0 and CHUNK % DIL_HALO == 0

    x = (x_prompt.reshape(t_p, D_MODEL), x_sample.reshape(t_s, D_MODEL))
    c_chunks = jnp.concatenate([jnp.repeat(c_prompt, s_p // CHUNK, axis=0),
                                jnp.repeat(c_sample, s_s // CHUNK, axis=0)], axis=0)
    nch = c_chunks.shape[0]
    mod = _ada(c_chunks, w_ada, b_ada).reshape(DEPTH, nch, 1, 6 * D_MODEL)
    flags = _chunk_flags(groups)

    perm_b, feat_b = _head_orders()
    slopes = 2.0 ** (-8.0 * np.arange(1, N_HEADS + 1) / N_HEADS)
    slopes_a = jnp.asarray(slopes.reshape(N_HEADS // 2, 2), _F32)
    slopes_b = jnp.asarray(slopes[perm_b].reshape(2, 8), _F32)
    qb0 = 3 * A_WIDTH
    in_cols = np.concatenate([np.arange(qb0), qb0 + feat_b, np.arange(qb0 + B_WIDTH, IN_WIDTH)])
    out_rows = np.concatenate([np.arange(A_WIDTH), A_WIDTH + feat_b])

    one_tile = jnp.zeros((t // FFN_TM,), jnp.int32), jnp.ones((t // FFN_TM,), jnp.int32)

    h = _modulate(x, mod[0])
    for l in range(DEPTH):
        mod_l = mod[l]
        w_in_l = w_in[l][:, in_cols].astype(_BF16)
        w_out_l = w_out[l][out_rows, :].astype(_BF16)
        g_l = g_out[l][out_rows]
        g_a, g_b = g_l[:A_WIDTH].reshape(1, A_WIDTH), g_l[A_WIDTH:].reshape(1, B_WIDTH)
        sink_b = attn_sink[l][perm_b].reshape(2, 8).astype(_F32)

        proj = _matmul(h, w_in_l)
        oa = _dilated_attention(proj, slopes_a, flags)
        ob = _swa_attention(proj, slopes_b, sink_b, flags)

        ln1 = ln1_g[l].reshape(1, D_MODEL), ln1_b[l].reshape(1, D_MODEL)
        ln2 = ln2_g[l].reshape(1, D_MODEL), ln2_b[l].reshape(1, D_MODEL)
        e = l // 2
        if l % 2 == 0:
            x1, h2 = _outproj(oa, ob, g_a, g_b, w_out_l, x, mod_l, *ln1)
            x2, h = _ffn(h2, ffn_w_gate[e:e + 1].astype(_BF16), ffn_w_up[e:e + 1].astype(_BF16),
                         ffn_w_down[e:e + 1].astype(_BF16), *one_tile, tm=FFN_TM,
                         ln=(x1, mod_l, *ln2, mod[l + 1] if l + 1 < DEPTH else None))
            x = (x2,)
        else:
            w_r = jnp.pad(moe_router[e], ((0, 0), (0, LANES - N_EXPERTS))).astype(_BF16)
            x, h2, rt = _outproj(oa, ob, g_a, g_b, w_out_l, x, mod_l, *ln1, w_router=w_r)
            src_token, pos_a, pos_b, tile_expert, tile_valid = _route(rt, MOE_TM)
            ys = _moe_ffn(h2, src_token, moe_w_gate[e].astype(_BF16), moe_w_up[e].astype(_BF16),
                          moe_w_down[e].astype(_BF16), tile_expert, tile_valid, MOE_TM)
            x = tuple(_combine_ln(ys, pos_a, pos_b, rt, x, mod_l, *ln2, row0=t0, n_rows=n * s)
                      for t0, n, s in groups)
            if l + 1 < DEPTH:
                h = _modulate(x, mod[l + 1])

    if len(x) == 1:
        x = (x[0][:t_p], x[0][t_p:])
    return x[0].reshape(nb_p, s_p, D_MODEL), x[1].reshape(nb_s, s_s, D_MODEL)
```
